```python
import math
import jax, jax.numpy as jnp
from jax import lax
import numpy as np

D_MODEL = 2048
BATCH = 2
SEQ = 8192
DEPTH = 2

MLA_WIDTH = D_MODEL // 2
S5_WIDTH = D_MODEL // 4
HY_WIDTH = D_MODEL // 4
MIX_WIDTH = MLA_WIDTH + S5_WIDTH + HY_WIDTH

MLA_V_DIM = 128
MLA_HEADS = MLA_WIDTH // MLA_V_DIM
MLA_NOPE_DIM = 128
MLA_ROPE_DIM = 64
MLA_Q_RANK = D_MODEL // 4
MLA_KV_RANK = D_MODEL // 8
ROPE_THETA = 10000.0
Q_BLOCK = 128
MAX_POS_OFFSET = 4096

S5_GROUP = 16
S5_GROUPS = S5_WIDTH // S5_GROUP
S5_STATE = 64
S5_DT_MIN = 1e-3
S5_DT_MAX = 1e-1

HY_EMB_BANDS = 16
HY_EMB_DIM = 1 + 2 * HY_EMB_BANDS
HY_FILTER_HIDDEN = 64
HY_DECAY_TARGET = 1e-2
HY_FAST_DECAY_PCT = 0.3
HY_SLOW_DECAY_PCT = 1.5

IN_COLS = MLA_Q_RANK + MLA_KV_RANK + MLA_ROPE_DIM + S5_WIDTH + 3 * HY_WIDTH
IN_SPLITS = (MLA_Q_RANK,
             MLA_Q_RANK + MLA_KV_RANK,
             MLA_Q_RANK + MLA_KV_RANK + MLA_ROPE_DIM,
             MLA_Q_RANK + MLA_KV_RANK + MLA_ROPE_DIM + S5_WIDTH)

D_FF = ((8 * D_MODEL // 3 + 255) // 256) * 256

ALPHA = (2 * DEPTH) ** 0.25
BETA = (8 * DEPTH) ** -0.25
LN_EPS = 1e-5
RMS_EPS = 1e-6
FILTER_EPS = 1e-6

kernel_name = 'hybrid_mla_s5_hyena_encoder'


def _layer_norm(x, g, b):
    xf = x.astype(jnp.float32)
    mu = jnp.mean(xf, axis=-1, keepdims=True)
    var = jnp.mean(jnp.square(xf - mu), axis=-1, keepdims=True)
    y = (xf - mu) * lax.rsqrt(var + LN_EPS)
    return (y * g.astype(jnp.float32) + b.astype(jnp.float32)).astype(x.dtype)


def _rms_norm(x, g):
    xf = x.astype(jnp.float32)
    y = xf * lax.rsqrt(jnp.mean(jnp.square(xf), axis=-1, keepdims=True) + RMS_EPS)
    return (y * g.astype(jnp.float32)).astype(x.dtype)


def _dwconv3(x, w, b):
    y = lax.conv_general_dilated(
        x, w.astype(x.dtype)[:, None, :], window_strides=(1,), padding=((1, 1),),
        dimension_numbers=('NWC', 'WIO', 'NWC'), feature_group_count=x.shape[-1])
    return y + b.astype(x.dtype)


def _rope(x, cos, sin):
    half = x.shape[-1] // 2
    x1, x2 = x[..., :half], x[..., half:]
    return jnp.concatenate([x1 * cos - x2 * sin, x2 * cos + x1 * sin], axis=-1)


def _mla(q_lat, kv_lat, k_rope, positions, q_norm_g, kv_norm_g, w_uq, w_ukv):
    bsz, seq, _ = q_lat.shape
    q = (_rms_norm(q_lat, q_norm_g) @ w_uq).reshape(bsz, seq, MLA_HEADS, MLA_NOPE_DIM + MLA_ROPE_DIM)
    kv = (_rms_norm(kv_lat, kv_norm_g) @ w_ukv).reshape(bsz, seq, MLA_HEADS, MLA_NOPE_DIM + MLA_V_DIM)
    q_nope, q_pe = q[..., :MLA_NOPE_DIM], q[..., MLA_NOPE_DIM:]
    k_nope, v = kv[..., :MLA_NOPE_DIM], kv[..., MLA_NOPE_DIM:]
    inv_freq = ROPE_THETA ** (-jnp.arange(0, MLA_ROPE_DIM, 2, dtype=jnp.float32) / MLA_ROPE_DIM)
    ang = positions.astype(jnp.float32)[..., None] * inv_freq
    cos = jnp.cos(ang).astype(q.dtype)
    sin = jnp.sin(ang).astype(q.dtype)
    q_pe = _rope(q_pe, cos[:, :, None, :], sin[:, :, None, :])
    k_pe = _rope(k_rope, cos, sin)
    q = jnp.concatenate([q_nope, q_pe], axis=-1)
    k = jnp.concatenate(
        [k_nope, jnp.broadcast_to(k_pe[:, :, None, :], (bsz, seq, MLA_HEADS, MLA_ROPE_DIM))], axis=-1)
    scale = (MLA_NOPE_DIM + MLA_ROPE_DIM) ** -0.5
    n_blocks = seq // Q_BLOCK
    q_blocks = q.reshape(bsz, n_blocks, Q_BLOCK, MLA_HEADS, -1).transpose(1, 0, 2, 3, 4)

    def attend(qb):
        s = jnp.einsum('bqhd,bkhd->bhqk', qb, k).astype(jnp.float32) * scale
        p = jax.nn.softmax(s, axis=-1).astype(v.dtype)
        return jnp.einsum('bhqk,bkhd->bqhd', p, v)

    o = lax.map(attend, q_blocks)
    return o.transpose(1, 0, 2, 3, 4).reshape(bsz, seq, MLA_HEADS * MLA_V_DIM)


def _complex_affine_combine(e1, e2):
    a1r, a1i, b1r, b1i = e1
    a2r, a2i, b2r, b2i = e2
    return (a2r * a1r - a2i * a1i,
            a2r * a1i + a2i * a1r,
            a2r * b1r - a2i * b1i + b2r,
            a2r * b1i + a2i * b1r + b2i)


def _s5(u, a_re, a_im, log_step, b_re, b_im, c_re, c_im, d, glu_w, glu_b):
    bsz, seq, _ = u.shape
    f32 = jnp.float32
    uf = u.astype(f32).reshape(bsz, seq, S5_GROUPS, S5_GROUP)
    y = uf * d.astype(f32).reshape(S5_GROUPS, S5_GROUP)
    for direction in range(2):
        ar = a_re[direction].astype(f32)
        ai = a_im[direction].astype(f32)
        step = jnp.exp(log_step[direction].astype(f32))[:, None]
        mag = jnp.exp(step * ar)
        abar_r = mag * jnp.cos(step * ai)
        abar_i = mag * jnp.sin(step * ai)
        den = ar * ar + ai * ai
        nr = abar_r - 1.0
        fr = (nr * ar + abar_i * ai) / den
        fi = (abar_i * ar - nr * ai) / den
        bu_r = jnp.einsum('blgh,gph->blgp', uf, b_re[direction].astype(f32))
        bu_i = jnp.einsum('blgh,gph->blgp', uf, b_im[direction].astype(f32))
        in_r = fr * bu_r - fi * bu_i
        in_i = fr * bu_i + fi * bu_r
        shape = in_r.shape
        _, _, s_r, s_i = lax.associative_scan(
            _complex_affine_combine,
            (jnp.broadcast_to(abar_r, shape), jnp.broadcast_to(abar_i, shape), in_r, in_i),
            axis=1, reverse=(direction == 1))
        y = y + (jnp.einsum('blgp,ghp->blgh', s_r, c_re[direction].astype(f32))
                 - jnp.einsum('blgp,ghp->blgh', s_i, c_im[direction].astype(f32)))
    y = jax.nn.gelu(y.reshape(bsz, seq, S5_WIDTH))
    y = y * jax.nn.sigmoid(y @ glu_w.astype(f32) + glu_b.astype(f32))
    return y.astype(u.dtype)


def _hyena_filter(seq, w1, b1, freq, w2, b2, w3):
    f32 = jnp.float32
    t01 = jnp.linspace(0.0, 1.0, seq, dtype=f32)[:, None]
    w = 2.0 * math.pi * jnp.arange(seq, dtype=f32)[:, None] / seq
    bands = jnp.linspace(1e-4, HY_EMB_BANDS - 1, HY_EMB_BANDS, dtype=f32)[None, :]
    fw = w * bands
    feat = jnp.concatenate([t01, jnp.cos(fw), -jnp.sin(fw)], axis=-1)
    h = jnp.sin(freq[0].astype(f32) * (feat @ w1.astype(f32) + b1.astype(f32)))
    h = jnp.sin(freq[1].astype(f32) * (h @ w2.astype(f32) + b2.astype(f32)))
    h = (h @ w3.astype(f32)).reshape(seq, 2, HY_WIDTH)
    deltas = jnp.linspace(math.log(HY_DECAY_TARGET) / HY_SLOW_DECAY_PCT,
                          math.log(HY_DECAY_TARGET) / HY_FAST_DECAY_PCT, HY_WIDTH, dtype=f32)
    h = h * jnp.exp(-t01 * jnp.abs(deltas))[:, None, :]
    filt = jnp.concatenate([h[:, 0], jnp.zeros((1, HY_WIDTH), f32), h[:0:-1, 1]], axis=0)
    return filt * lax.rsqrt(jnp.sum(jnp.square(filt), axis=0, keepdims=True) + FILTER_EPS)


def _hyena(u_hy, conv_w, conv_b, f_w1, f_b1, f_freq, f_w2, f_b2, f_w3, f_bias):
    bsz, seq, _ = u_hy.shape
    z = _dwconv3(u_hy, conv_w, conv_b)
    x0, x1, v = jnp.split(z, 3, axis=-1)
    filt = _hyena_filter(seq, f_w1, f_b1, f_freq, f_w2, f_b2, f_w3)
    gv = (x1 * v).astype(jnp.float32)
    n_fft = 2 * seq
    y = jnp.fft.irfft(jnp.fft.rfft(gv, n=n_fft, axis=1) * jnp.fft.rfft(filt, n=n_fft, axis=0)[None],
                      n=n_fft, axis=1)[:, :seq]
    y = y + gv * f_bias.astype(jnp.float32)
    return (x0.astype(jnp.float32) * y).astype(u_hy.dtype)


def _conv_ffn(u, w_gate, w_up, conv_w, conv_b, w_down):
    g = _dwconv3(u @ w_gate, conv_w, conv_b)
    return (jax.nn.silu(g) * (u @ w_up)) @ w_down


def setup_inputs(seed: int = 0) -> dict:
    key = jax.random.key(seed)
    ks = iter(jax.random.split(key, 48))
    f32 = jnp.float32

    def nrm(shape, scale):
        return jax.random.normal(next(ks), shape, f32) * scale

    def gain(shape):
        return 1.0 + nrm(shape, 0.02)

    x = nrm((BATCH, SEQ, D_MODEL), 1.0)
    c = nrm((BATCH, D_MODEL), 1.0)
    offsets = jax.random.randint(next(ks), (BATCH, 1), 0, MAX_POS_OFFSET, dtype=jnp.int32)
    positions = offsets + jnp.arange(SEQ, dtype=jnp.int32)[None, :]

    ada_w = nrm((DEPTH, D_MODEL, 6 * D_MODEL), D_MODEL ** -0.5)
    ada_b = nrm((DEPTH, 6 * D_MODEL), 0.02)
    w_in = nrm((DEPTH, D_MODEL, IN_COLS), D_MODEL ** -0.5)
    q_norm_g = gain((DEPTH, MLA_Q_RANK))
    kv_norm_g = gain((DEPTH, MLA_KV_RANK))
    w_uq = nrm((DEPTH, MLA_Q_RANK, MLA_HEADS * (MLA_NOPE_DIM + MLA_ROPE_DIM)), MLA_Q_RANK ** -0.5)
    w_ukv = nrm((DEPTH, MLA_KV_RANK, MLA_HEADS * (MLA_NOPE_DIM + MLA_V_DIM)), MLA_KV_RANK ** -0.5)

    s5_a_re = -0.5 * (1.0 + nrm((DEPTH, 2, S5_GROUPS, S5_STATE), 0.01))
    s5_a_im = math.pi * jnp.arange(S5_STATE, dtype=f32) + nrm((DEPTH, 2, S5_GROUPS, S5_STATE), 0.01)
    s5_log_step = jax.random.uniform(next(ks), (DEPTH, 2, S5_GROUPS), f32,
                                     math.log(S5_DT_MIN), math.log(S5_DT_MAX))
    s5_b_re = nrm((DEPTH, 2, S5_GROUPS, S5_STATE, S5_GROUP), (2 * S5_GROUP) ** -0.5)
    s5_b_im = nrm((DEPTH, 2, S5_GROUPS, S5_STATE, S5_GROUP), (2 * S5_GROUP) ** -0.5)
    s5_c_re = nrm((DEPTH, 2, S5_GROUPS, S5_GROUP, S5_STATE), S5_STATE ** -0.5)
    s5_c_im = nrm((DEPTH, 2, S5_GROUPS, S5_GROUP, S5_STATE), S5_STATE ** -0.5)
    s5_d = nrm((DEPTH, S5_WIDTH), 1.0)
    s5_glu_w = nrm((DEPTH, S5_WIDTH, S5_WIDTH), S5_WIDTH ** -0.5)
    s5_glu_b = nrm((DEPTH, S5_WIDTH), 0.02)

    hy_conv_w = nrm((DEPTH, 3, 3 * HY_WIDTH), 3 ** -0.5)
    hy_conv_b = nrm((DEPTH, 3 * HY_WIDTH), 0.02)
    hy_f_w1 = nrm((DEPTH, HY_EMB_DIM, HY_FILTER_HIDDEN), HY_EMB_DIM ** -0.5)
    hy_f_b1 = nrm((DEPTH, HY_FILTER_HIDDEN), 0.02)
    hy_f_freq = gain((DEPTH, 2, HY_FILTER_HIDDEN))
    hy_f_w2 = nrm((DEPTH, HY_FILTER_HIDDEN, HY_FILTER_HIDDEN), HY_FILTER_HIDDEN ** -0.5)
    hy_f_b2 = nrm((DEPTH, HY_FILTER_HIDDEN), 0.02)
    hy_f_w3 = nrm((DEPTH, HY_FILTER_HIDDEN, 2 * HY_WIDTH), HY_FILTER_HIDDEN ** -0.5)
    hy_f_bias = nrm((DEPTH, HY_WIDTH), 0.1)

    w_out = nrm((DEPTH, MIX_WIDTH, D_MODEL), BETA * MIX_WIDTH ** -0.5)
    ln1_g = gain((DEPTH, D_MODEL))
    ln1_b = nrm((DEPTH, D_MODEL), 0.02)
    ffn_w_gate = nrm((DEPTH, D_MODEL, D_FF), D_MODEL ** -0.5)
    ffn_w_up = nrm((DEPTH, D_MODEL, D_FF), D_MODEL ** -0.5)
    ffn_conv_w = nrm((DEPTH, 3, D_FF), 3 ** -0.5)
    ffn_conv_b = nrm((DEPTH, D_FF), 0.02)
    ffn_w_down = nrm((DEPTH, D_FF, D_MODEL), BETA * D_FF ** -0.5)
    ln2_g = gain((DEPTH, D_MODEL))
    ln2_b = nrm((DEPTH, D_MODEL), 0.02)
    return {
        'x': x, 'c': c, 'positions': positions,
        'ada_w': ada_w, 'ada_b': ada_b, 'w_in': w_in,
        'q_norm_g': q_norm_g, 'kv_norm_g': kv_norm_g, 'w_uq': w_uq, 'w_ukv': w_ukv,
        's5_a_re': s5_a_re, 's5_a_im': s5_a_im, 's5_log_step': s5_log_step,
        's5_b_re': s5_b_re, 's5_b_im': s5_b_im, 's5_c_re': s5_c_re, 's5_c_im': s5_c_im,
        's5_d': s5_d, 's5_glu_w': s5_glu_w, 's5_glu_b': s5_glu_b,
        'hy_conv_w': hy_conv_w, 'hy_conv_b': hy_conv_b,
        'hy_f_w1': hy_f_w1, 'hy_f_b1': hy_f_b1, 'hy_f_freq': hy_f_freq,
        'hy_f_w2': hy_f_w2, 'hy_f_b2': hy_f_b2, 'hy_f_w3': hy_f_w3, 'hy_f_bias': hy_f_bias,
        'w_out': w_out, 'ln1_g': ln1_g, 'ln1_b': ln1_b,
        'ffn_w_gate': ffn_w_gate, 'ffn_w_up': ffn_w_up, 'ffn_conv_w': ffn_conv_w,
        'ffn_conv_b': ffn_conv_b, 'ffn_w_down': ffn_w_down, 'ln2_g': ln2_g, 'ln2_b': ln2_b,
    }


def reference(x, c, positions, ada_w, ada_b, w_in, q_norm_g, kv_norm_g, w_uq, w_ukv,
              s5_a_re, s5_a_im, s5_log_step, s5_b_re, s5_b_im, s5_c_re, s5_c_im,
              s5_d, s5_glu_w, s5_glu_b, hy_conv_w, hy_conv_b, hy_f_w1, hy_f_b1, hy_f_freq,
              hy_f_w2, hy_f_b2, hy_f_w3, hy_f_bias, w_out, ln1_g, ln1_b,
              ffn_w_gate, ffn_w_up, ffn_conv_w, ffn_conv_b, ffn_w_down, ln2_g, ln2_b):
    cond = jax.nn.silu(c)
    for l in range(DEPTH):
        mod = cond @ ada_w[l] + ada_b[l]
        sh_m, sc_m, g_m, sh_f, sc_f, g_f = [m[:, None, :] for m in jnp.split(mod, 6, axis=-1)]

        u = x * (1.0 + sc_m) + sh_m
        proj = u @ w_in[l]
        q_lat, kv_lat, k_rope, u_s5, u_hy = jnp.split(proj, IN_SPLITS, axis=-1)
        o_mla = _mla(q_lat, kv_lat, k_rope, positions, q_norm_g[l], kv_norm_g[l], w_uq[l], w_ukv[l])
        o_s5 = _s5(u_s5, s5_a_re[l], s5_a_im[l], s5_log_step[l], s5_b_re[l], s5_b_im[l],
                   s5_c_re[l], s5_c_im[l], s5_d[l], s5_glu_w[l], s5_glu_b[l])
        o_hy = _hyena(u_hy, hy_conv_w[l], hy_conv_b[l], hy_f_w1[l], hy_f_b1[l], hy_f_freq[l],
                      hy_f_w2[l], hy_f_b2[l], hy_f_w3[l], hy_f_bias[l])
        mixed = jnp.concatenate([o_mla, o_s5, o_hy], axis=-1) @ w_out[l]
        x = _layer_norm(ALPHA * x + g_m * mixed, ln1_g[l], ln1_b[l])

        u = x * (1.0 + sc_f) + sh_f
        f = _conv_ffn(u, ffn_w_gate[l], ffn_w_up[l], ffn_conv_w[l], ffn_conv_b[l], ffn_w_down[l])
        x = _layer_norm(ALPHA * x + g_f * f, ln2_g[l], ln2_b[l])
    return x
```

```python
import functools
import math

import numpy as np
import jax
import jax.numpy as jnp
from jax import lax
from jax.experimental import pallas as pl
from jax.experimental.pallas import tpu as pltpu

F32 = jnp.float32
BF16 = jnp.bfloat16

MLA_HEADS = 8
MLA_NOPE = 128
MLA_ROPE = 64
MLA_V = 128
MLA_QK = MLA_NOPE + MLA_ROPE
MLA_Q_RANK = 512
MLA_KV_RANK = 256
ROPE_THETA = 10000.0
S5_W = 512
S5_GROUP = 16
S5_GROUPS = 32
S5_STATE = 64
S5_CHUNK = 16
S5_PAIR = 2
HY_W = 512
HY_BANDS = 16
HY_DECAY_TARGET = 1e-2
HY_FAST_PCT = 0.3
HY_SLOW_PCT = 1.5
LN_EPS = 1e-5
RMS_EPS = 1e-6
FILTER_EPS = 1e-6
DEPTH = 2
ALPHA = (2 * DEPTH) ** 0.25

COL_Q = 0
COL_S5 = 512
COL_HY = 1024
COL_KV = 2560
COL_KR = 2816
IN_COLS_PAD = 2944

LANE = 128
SUBLANE = 8
VMEM_LIMIT = 56 * 1024 * 1024


def _cp(*sem):
    return pltpu.CompilerParams(dimension_semantics=sem, vmem_limit_bytes=VMEM_LIMIT)


def _layer_norm(y, g, b):
    mu = jnp.mean(y, axis=-1, keepdims=True)
    d = y - mu
    var = jnp.mean(d * d, axis=-1, keepdims=True)
    return d * lax.rsqrt(var + LN_EPS) * g + b


def _ada_kernel(c_ref, w_ref, b_ref, o_ref):
    c = c_ref[...]
    cond = c * jax.nn.sigmoid(c)
    o_ref[0] = jnp.dot(cond, w_ref[0], preferred_element_type=F32) + b_ref[0]


def _ada(c, ada_w, ada_b):
    bsz, d = c.shape
    depth, _, n = ada_w.shape
    tn = 1024
    cp = jnp.zeros((SUBLANE, d), F32).at[:bsz].set(c)
    out = pl.pallas_call(
        _ada_kernel,
        grid=(depth, n // tn),
        in_specs=[pl.BlockSpec((SUBLANE, d), lambda l, j: (0, 0)),
                  pl.BlockSpec((1, d, tn), lambda l, j: (l, 0, j)),
                  pl.BlockSpec((1, 1, tn), lambda l, j: (l, 0, j))],
        out_specs=pl.BlockSpec((1, SUBLANE, tn), lambda l, j: (l, 0, j)),
        out_shape=jax.ShapeDtypeStruct((depth, SUBLANE, n), F32),
        compiler_params=_cp("parallel", "parallel"),
        name="ada_mod",
    )(cp, ada_w, ada_b.reshape(depth, 1, n))
    return out[:, :bsz].reshape(depth, bsz, 6, d)


def _inproj_kernel(x_ref, mod_ref, w_ref, o_ref):
    m = mod_ref[0]
    u = x_ref[0] * (1.0 + m[1:2]) + m[0:1]
    o_ref[0] = jnp.dot(u.astype(BF16), w_ref[...], preferred_element_type=F32)


def _inproj(x, mod, w):
    bsz, seq, d = x.shape
    n = w.shape[1]
    tm = min(256, seq)
    return pl.pallas_call(
        _inproj_kernel,
        grid=(bsz, seq // tm),
        in_specs=[pl.BlockSpec((1, tm, d), lambda b, i: (b, i, 0)),
                  pl.BlockSpec((1, 6, d), lambda b, i: (b, 0, 0)),
                  pl.BlockSpec((d, n), lambda b, i: (0, 0))],
        out_specs=pl.BlockSpec((1, tm, n), lambda b, i: (b, i, 0)),
        out_shape=jax.ShapeDtypeStruct((bsz, seq, n), F32),
        compiler_params=_cp("parallel", "parallel"),
        name="in_proj",
    )(x, mod, w)


def _mla_prep_kernel(ql_ref, kvl_ref, kr_ref, pos_ref, invf_ref, qg_ref, kvg_ref,
                     wq_ref, wkv_ref, q_ref, k_ref, v_ref):
    scale = MLA_QK ** -0.5
    ql = ql_ref[0]
    qn = ql * lax.rsqrt(jnp.mean(ql * ql, axis=-1, keepdims=True) + RMS_EPS) * qg_ref[...]
    kvl = kvl_ref[0]
    kvn = kvl * lax.rsqrt(jnp.mean(kvl * kvl, axis=-1, keepdims=True) + RMS_EPS) * kvg_ref[...]
    ang = pos_ref[0] * invf_ref[...]
    cos = jnp.cos(ang)
    sin = jnp.sin(ang)
    qa = jnp.dot(qn.astype(BF16), wq_ref[...], preferred_element_type=F32)
    kva = jnp.dot(kvn.astype(BF16), wkv_ref[...], preferred_element_type=F32)
    kr = kr_ref[0]
    kpe = (kr * cos + pltpu.roll(kr, MLA_ROPE, 1) * sin)[:, :MLA_ROPE].astype(BF16)
    for h in range(MLA_HEADS):
        c0 = 2 * LANE * h
        pr = qa[:, c0 + LANE:c0 + 2 * LANE]
        qpe = pr * cos + pltpu.roll(pr, MLA_ROPE, 1) * sin
        q_ref[0, h, :, 0:MLA_NOPE] = (qa[:, c0:c0 + LANE] * scale).astype(BF16)
        q_ref[0, h, :, MLA_NOPE:MLA_QK] = (qpe[:, :MLA_ROPE] * scale).astype(BF16)
        k_ref[0, h, :, 0:MLA_NOPE] = kva[:, c0:c0 + LANE].astype(BF16)
        k_ref[0, h, :, MLA_NOPE:MLA_QK] = kpe
        v_ref[0, h] = kva[:, c0 + LANE:c0 + 2 * LANE].astype(BF16)


def _mla_prep(proj, positions, q_g, kv_g, wq, wkv):
    bsz, seq, _ = proj.shape
    tm = min(512, seq)
    inv_freq = ROPE_THETA ** (-jnp.arange(0, MLA_ROPE, 2, dtype=F32) / MLA_ROPE)
    invf = jnp.tile(inv_freq, LANE // (MLA_ROPE // 2)).reshape(1, LANE)
    pos = positions.astype(F32).reshape(bsz, seq, 1)
    hshape = lambda w: jax.ShapeDtypeStruct((bsz, MLA_HEADS, seq, w), BF16)
    hspec = lambda w: pl.BlockSpec((1, MLA_HEADS, tm, w), lambda b, i: (b, 0, i, 0))
    const = lambda shp: pl.BlockSpec(shp, lambda b, i: (0,) * len(shp))
    return pl.pallas_call(
        _mla_prep_kernel,
        grid=(bsz, seq // tm),
        in_specs=[pl.BlockSpec((1, tm, MLA_Q_RANK), lambda b, i: (b, i, COL_Q // MLA_Q_RANK)),
                  pl.BlockSpec((1, tm, MLA_KV_RANK), lambda b, i: (b, i, COL_KV // MLA_KV_RANK)),
                  pl.BlockSpec((1, tm, LANE), lambda b, i: (b, i, COL_KR // LANE)),
                  pl.BlockSpec((1, tm, 1), lambda b, i: (b, i, 0)),
                  const((1, LANE)), const((1, MLA_Q_RANK)), const((1, MLA_KV_RANK)),
                  const(wq.shape), const(wkv.shape)],
        out_specs=[hspec(MLA_QK), hspec(MLA_QK), hspec(MLA_V)],
        out_shape=[hshape(MLA_QK), hshape(MLA_QK), hshape(MLA_V)],
        compiler_params=_cp("parallel", "parallel"),
        name="mla_prep",
    )(proj, proj, proj, pos, invf, q_g.reshape(1, -1), kv_g.reshape(1, -1), wq, wkv)


def _attn_kernel(q_ref, k_ref, v_ref, o_ref, *, tk):
    q = q_ref[0, 0]
    tq = q.shape[0]
    nk = k_ref.shape[2] // tk

    def body(j, carry):
        m, l, acc = carry
        off = pl.multiple_of(j * tk, tk)
        kj = k_ref[0, 0, pl.ds(off, tk), :]
        vj = v_ref[0, 0, pl.ds(off, tk), :]
        s = lax.dot_general(q, kj, (((1,), (1,)), ((), ())), preferred_element_type=F32)
        m_new = jnp.maximum(m, jnp.max(s, axis=-1, keepdims=True))
        a = jnp.exp(m - m_new)
        p = jnp.exp(s - m_new)
        l = a * l + jnp.sum(p, axis=-1, keepdims=True)
        acc = a * acc + jnp.dot(p.astype(BF16), vj, preferred_element_type=F32)
        return m_new, l, acc

    init = (jnp.full((tq, 1), jnp.finfo(F32).min, F32), jnp.zeros((tq, 1), F32),
            jnp.zeros((tq, MLA_V), F32))
    _, l, acc = lax.fori_loop(0, nk, body, init)
    o_ref[0] = (acc / l).astype(o_ref.dtype)


def _attention(q, k, v):
    bsz, nh, seq, _ = q.shape
    tq = min(512, seq)
    tk = min(512, seq)
    return pl.pallas_call(
        functools.partial(_attn_kernel, tk=tk),
        grid=(bsz, nh, seq // tq),
        in_specs=[pl.BlockSpec((1, 1, tq, MLA_QK), lambda b, h, i: (b, h, i, 0)),
                  pl.BlockSpec((1, 1, seq, MLA_QK), lambda b, h, i: (b, h, 0, 0)),
                  pl.BlockSpec((1, 1, seq, MLA_V), lambda b, h, i: (b, h, 0, 0))],
        out_specs=pl.BlockSpec((1, tq, MLA_V), lambda b, h, i: (b, i, h)),
        out_shape=jax.ShapeDtypeStruct((bsz, seq, nh * MLA_V), BF16),
        compiler_params=_cp("parallel", "parallel", "parallel"),
        name="mla_attention",
    )(q, k, v)


def _s5_params(a_re, a_im, log_step, b_re, b_im, c_re, c_im, d):
    q = S5_CHUNK
    g, p, hh = S5_GROUPS, S5_STATE, S5_GROUP
    ks = jnp.arange(q + 1, dtype=F32)[:, None, None]
    out = {}
    lag_k = []
    for direction in range(2):
        ar, ai = a_re[direction].astype(F32), a_im[direction].astype(F32)
        step = jnp.exp(log_step[direction].astype(F32))[:, None]
        mag = jnp.exp(step * ar)
        abar_r = mag * jnp.cos(step * ai)
        abar_i = mag * jnp.sin(step * ai)
        den = ar * ar + ai * ai
        nr = abar_r - 1.0
        fr = (nr * ar + abar_i * ai) / den
        fi = (abar_i * ar - nr * ai) / den
        pmag = jnp.exp(ks * (step * ar)[None])
        pw_r = pmag * jnp.cos(ks * (step * ai)[None])
        pw_i = pmag * jnp.sin(ks * (step * ai)[None])
        br, bi = b_re[direction].astype(F32), b_im[direction].astype(F32)
        bf_r = fr[..., None] * br - fi[..., None] * bi
        bf_i = fr[..., None] * bi + fi[..., None] * br
        cr, ci = c_re[direction].astype(F32), c_im[direction].astype(F32)
        cp_r = cr[None] * pw_r[:, :, None, :] - ci[None] * pw_i[:, :, None, :]
        cp_i = cr[None] * pw_i[:, :, None, :] + ci[None] * pw_r[:, :, None, :]
        lag = (jnp.einsum('kghp,gpj->kghj', cp_r[:q], bf_r, precision='highest')
               - jnp.einsum('kghp,gpj->kghj', cp_i[:q], bf_i, precision='highest'))
        lag_k.append(lag)
        tpow = (q - 1 - jnp.arange(q)) if direction == 0 else jnp.arange(q)
        wr = pw_r[tpow][:, :, :, None] * bf_r[None] - pw_i[tpow][:, :, :, None] * bf_i[None]
        wi = pw_r[tpow][:, :, :, None] * bf_i[None] + pw_i[tpow][:, :, :, None] * bf_r[None]
        out[f'wst_r{direction}'] = wr.transpose(1, 0, 3, 2).reshape(g, q * hh, p)
        out[f'wst_i{direction}'] = wi.transpose(1, 0, 3, 2).reshape(g, q * hh, p)
        opow = (jnp.arange(q) + 1) if direction == 0 else (q - jnp.arange(q))
        out[f'cp_r{direction}'] = cp_r[opow].transpose(1, 3, 0, 2).reshape(g, p, q * hh)
        out[f'cp_i{direction}'] = (-cp_i[opow]).transpose(1, 3, 0, 2).reshape(g, p, q * hh)
        out[f'a_r{direction}'] = pw_r[q].reshape(1, g * p)
        out[f'a_i{direction}'] = pw_i[q].reshape(1, g * p)
    t_in = jnp.arange(q)[:, None]
    t_out = jnp.arange(q)[None, :]
    dlt = t_out - t_in
    kf = lag_k[0][jnp.clip(dlt, 0, q - 1)]
    kb = lag_k[1][jnp.clip(-dlt, 0, q - 1)]
    kf = jnp.where((dlt >= 0)[:, :, None, None, None], kf, 0.0)
    kb = jnp.where((dlt <= 0)[:, :, None, None, None], kb, 0.0)
    dmat = (d.astype(F32).reshape(g, hh)[None, None, :, :, None]
            * jnp.eye(hh, dtype=F32)[None, None, None]
            * jnp.eye(q, dtype=F32)[:, :, None, None, None])
    m = kf + kb + dmat
    out['m'] = m.transpose(2, 0, 4, 1, 3).reshape(g, q * hh, q * hh)
    return out


def _pair_block_diag(w):
    g, r, c = w.shape
    w = w.reshape(g // S5_PAIR, S5_PAIR, r, c)
    eye = jnp.eye(S5_PAIR, dtype=w.dtype)
    return jnp.einsum('njrc,jk->njrkc', w, eye).reshape(g // S5_PAIR, S5_PAIR * r, S5_PAIR * c)


def _s5_state_kernel(x_ref, w_ref, vfr_ref, vfi_ref, vbr_ref, vbi_ref):
    v = jnp.dot(x_ref[0], w_ref[0], preferred_element_type=F32)
    vfr_ref[...] = v[:, 0 * LANE:1 * LANE]
    vfi_ref[...] = v[:, 1 * LANE:2 * LANE]
    vbr_ref[...] = v[:, 2 * LANE:3 * LANE]
    vbi_ref[...] = v[:, 3 * LANE:4 * LANE]


def _s5_scan_kernel(vfr_ref, vfi_ref, vbr_ref, vbi_ref, afr_ref, afi_ref, abr_ref, abi_ref,
                    sfr_ref, sfi_ref, sbr_ref, sbi_ref):
    nc = vfr_ref.shape[1]
    afr, afi, abr, abi = afr_ref[...], afi_ref[...], abr_ref[...], abi_ref[...]

    def body(c, carry):
        fr, fi, br, bi = carry
        cb = nc - 1 - c
        sfr_ref[0, pl.ds(c, 1), :] = fr
        sfi_ref[0, pl.ds(c, 1), :] = fi
        sbr_ref[0, pl.ds(cb, 1), :] = br
        sbi_ref[0, pl.ds(cb, 1), :] = bi
        nfr = afr * fr - afi * fi + vfr_ref[0, pl.ds(c, 1), :]
        nfi = afr * fi + afi * fr + vfi_ref[0, pl.ds(c, 1), :]
        nbr = abr * br - abi * bi + vbr_ref[0, pl.ds(cb, 1), :]
        nbi = abr * bi + abi * br + vbi_ref[0, pl.ds(cb, 1), :]
        return nfr, nfi, nbr, nbi

    z = jnp.zeros((1, vfr_ref.shape[2]), F32)
    lax.fori_loop(0, nc, body, (z, z, z, z))


def _s5_out_kernel(x_ref, m_ref, sfr_ref, sfi_ref, sbr_ref, sbi_ref,
                   cfr_ref, cfi_ref, cbr_ref, cbi_ref, y_ref):
    y = jnp.dot(x_ref[0], m_ref[0], preferred_element_type=F32)
    y += jnp.dot(sfr_ref[...].astype(BF16), cfr_ref[0], preferred_element_type=F32)
    y += jnp.dot(sfi_ref[...].astype(BF16), cfi_ref[0], preferred_element_type=F32)
    y += jnp.dot(sbr_ref[...].astype(BF16), cbr_ref[0], preferred_element_type=F32)
    y += jnp.dot(sbi_ref[...].astype(BF16), cbi_ref[0], preferred_element_type=F32)
    y_ref[0] = y


def _s5_post_kernel(y_ref, w_ref, b_ref, o_ref):
    y = jax.nn.gelu(y_ref[0])
    z = jnp.dot(y.astype(BF16), w_ref[...], preferred_element_type=F32) + b_ref[...]
    o_ref[0] = (y * jax.nn.sigmoid(z)).astype(o_ref.dtype)


def _s5(proj, prm, glu_w, glu_b):
    bsz, seq, _ = proj.shape
    q, hh, g, p = S5_CHUNK, S5_GROUP, S5_GROUPS, S5_STATE
    npair = g // S5_PAIR
    nc = seq // q
    rows = bsz * nc
    feat = S5_PAIR * q * hh
    u = proj[:, :, COL_S5:COL_S5 + S5_W]
    x = (u.reshape(bsz, nc, q, npair, S5_PAIR, hh).transpose(3, 0, 1, 4, 2, 5)
         .reshape(npair, rows, feat).astype(BF16))

    def cat_state(d):
        return jnp.concatenate([_pair_block_diag(prm[f'wst_r{d}']),
                                _pair_block_diag(prm[f'wst_i{d}'])], axis=-1)
    wst = jnp.concatenate([cat_state(0), cat_state(1)], axis=-1).astype(BF16)
    nst = g * p
    vshape = jax.ShapeDtypeStruct((rows, nst), F32)
    vspec = pl.BlockSpec((rows, LANE), lambda n: (0, n))
    v4 = pl.pallas_call(
        _s5_state_kernel,
        grid=(npair,),
        in_specs=[pl.BlockSpec((1, rows, feat), lambda n: (n, 0, 0)),
                  pl.BlockSpec((1, feat, 4 * LANE), lambda n: (n, 0, 0))],
        out_specs=[vspec] * 4,
        out_shape=[vshape] * 4,
        compiler_params=_cp("parallel"),
        name="s5_state",
    )(x, wst)

    lt = min(512, nst)
    v4 = [v.reshape(bsz, nc, nst) for v in v4]
    sspec = pl.BlockSpec((1, nc, lt), lambda b, n: (b, 0, n))
    aspec = pl.BlockSpec((1, lt), lambda b, n: (0, n))
    s4 = pl.pallas_call(
        _s5_scan_kernel,
        grid=(bsz, nst // lt),
        in_specs=[sspec] * 4 + [aspec] * 4,
        out_specs=[sspec] * 4,
        out_shape=[jax.ShapeDtypeStruct((bsz, nc, nst), F32)] * 4,
        compiler_params=_cp("parallel", "parallel"),
        name="s5_scan",
    )(*v4, prm['a_r0'], prm['a_i0'], prm['a_r1'], prm['a_i1'])
    s4 = [s.reshape(rows, nst) for s in s4]

    mm = _pair_block_diag(prm['m']).astype(BF16)
    cps = [_pair_block_diag(prm[k]).astype(BF16) for k in ('cp_r0', 'cp_i0', 'cp_r1', 'cp_i1')]
    cspec = pl.BlockSpec((1, LANE, feat), lambda n: (n, 0, 0))
    y = pl.pallas_call(
        _s5_out_kernel,
        grid=(npair,),
        in_specs=[pl.BlockSpec((1, rows, feat), lambda n: (n, 0, 0)),
                  pl.BlockSpec((1, feat, feat), lambda n: (n, 0, 0))] + [vspec] * 4 + [cspec] * 4,
        out_specs=pl.BlockSpec((1, rows, feat), lambda n: (n, 0, 0)),
        out_shape=jax.ShapeDtypeStruct((npair, rows, feat), F32),
        compiler_params=_cp("parallel"),
        name="s5_out",
    )(x, mm, *s4, *cps)
    y = (y.reshape(npair, bsz, nc, S5_PAIR, q, hh).transpose(1, 2, 4, 0, 3, 5)
         .reshape(bsz, seq, S5_W))

    tm = min(1024, seq)
    return pl.pallas_call(
        _s5_post_kernel,
        grid=(bsz, seq // tm),
        in_specs=[pl.BlockSpec((1, tm, S5_W), lambda b, i: (b, i, 0)),
                  pl.BlockSpec((S5_W, S5_W), lambda b, i: (0, 0)),
                  pl.BlockSpec((1, S5_W), lambda b, i: (0, 0))],
        out_specs=pl.BlockSpec((1, tm, S5_W), lambda b, i: (b, i, 0)),
        out_shape=jax.ShapeDtypeStruct((bsz, seq, S5_W), BF16),
        compiler_params=_cp("parallel", "parallel"),
        name="s5_post",
    )(y, glu_w.astype(BF16), glu_b.reshape(1, -1).astype(F32))


def _hy_prep_kernel(x0_ref, x1_ref, v_ref, w_ref, b_ref, x0c_ref, gv_ref):
    seq = x0_ref.shape[1]
    row = lax.broadcasted_iota(jnp.int32, (seq, LANE), 0)
    first = row == 0
    last = row == seq - 1

    def conv(ref, k):
        u = ref[0]
        w = w_ref[k]
        prev = jnp.where(first, 0.0, pltpu.roll(u, 1, 0))
        nxt = jnp.where(last, 0.0, pltpu.roll(u, seq - 1, 0))
        return prev * w[0:1] + u * w[1:2] + nxt * w[2:3] + b_ref[k]

    x0c_ref[0] = conv(x0_ref, 0)
    gv_ref[0] = conv(x1_ref, 1) * conv(v_ref, 2)


def _hy_prep(proj, conv_w, conv_b):
    bsz, seq, _ = proj.shape
    nt = HY_W // LANE
    base = COL_HY // LANE
    w = conv_w.astype(F32).reshape(3, 3, HY_W).transpose(1, 0, 2)
    b = conv_b.astype(F32).reshape(3, 1, HY_W)
    col = lambda part: pl.BlockSpec((1, seq, LANE), lambda bb, c: (bb, 0, base + part * nt + c))
    ospec = pl.BlockSpec((1, seq, LANE), lambda bb, c: (bb, 0, c))
    return pl.pallas_call(
        _hy_prep_kernel,
        grid=(bsz, nt),
        in_specs=[col(0), col(1), col(2),
                  pl.BlockSpec((3, 3, LANE), lambda bb, c: (0, 0, c)),
                  pl.BlockSpec((3, 1, LANE), lambda bb, c: (0, 0, c))],
        out_specs=[ospec, ospec],
        out_shape=[jax.ShapeDtypeStruct((bsz, seq, HY_W), F32)] * 2,
        compiler_params=_cp("parallel", "parallel"),
        name="hyena_prep",
    )(proj, proj, proj, w, b)


def _hy_filter_kernel(feat_ref, w1_ref, b1_ref, fq_ref, w2_ref, b2_ref, w3f_ref, w3b_ref,
                      dl_ref, o_ref):
    feat = feat_ref[...]
    fq = fq_ref[...]
    h = jnp.sin(fq[0:1] * (jnp.dot(feat, w1_ref[...], preferred_element_type=F32) + b1_ref[...]))
    h = jnp.sin(fq[1:2] * (jnp.dot(h, w2_ref[...], preferred_element_type=F32) + b2_ref[...]))
    win = jnp.exp(-feat[:, 0:1] * dl_ref[...])
    hf = jnp.dot(h, w3f_ref[...], preferred_element_type=F32) * win
    hb = jnp.dot(h, w3b_ref[...], preferred_element_type=F32) * win
    row = lax.broadcasted_iota(jnp.int32, hb.shape, 0)
    hb = jnp.where(row == 0, 0.0, hb)
    ss = jnp.sum(hf * hf, axis=0, keepdims=True) + jnp.sum(hb * hb, axis=0, keepdims=True)
    scale = lax.rsqrt(ss + FILTER_EPS)
    o_ref[0] = hf * scale
    o_ref[1] = hb * scale


def _hy_filter(seq, w1, b1, freq, w2, b2, w3):
    t01 = jnp.linspace(0.0, 1.0, seq, dtype=F32)[:, None]
    w = 2.0 * math.pi * jnp.arange(seq, dtype=F32)[:, None] / seq
    bands = jnp.linspace(1e-4, HY_BANDS - 1, HY_BANDS, dtype=F32)[None, :]
    fw = w * bands
    emb = 1 + 2 * HY_BANDS
    kpad = 64
    feat = jnp.concatenate([t01, jnp.cos(fw), -jnp.sin(fw), jnp.zeros((seq, kpad - emb), F32)], axis=-1)
    w1p = jnp.zeros((kpad, w1.shape[1]), F32).at[:emb].set(w1.astype(F32))
    deltas = jnp.abs(jnp.linspace(math.log(HY_DECAY_TARGET) / HY_SLOW_PCT,
                                  math.log(HY_DECAY_TARGET) / HY_FAST_PCT, HY_W, dtype=F32))[None, :]
    nt = HY_W // LANE
    hid = w2.shape[0]
    const = lambda shp: pl.BlockSpec(shp, lambda c: (0,) * len(shp))
    return pl.pallas_call(
        _hy_filter_kernel,
        grid=(nt,),
        in_specs=[const((seq, kpad)), const((kpad, hid)), const((1, hid)), const((2, hid)),
                  const((hid, hid)), const((1, hid)),
                  pl.BlockSpec((hid, LANE), lambda c: (0, c)),
                  pl.BlockSpec((hid, LANE), lambda c: (0, nt + c)),
                  pl.BlockSpec((1, LANE), lambda c: (0, c))],
        out_specs=pl.BlockSpec((2, seq, LANE), lambda c: (0, 0, c)),
        out_shape=jax.ShapeDtypeStruct((2, seq, HY_W), F32),
        compiler_params=_cp("parallel"),
        name="hyena_filter",
    )(feat, w1p, b1.reshape(1, -1).astype(F32), freq.astype(F32), w2.astype(F32),
      b2.reshape(1, -1).astype(F32), w3.astype(F32), w3.astype(F32), deltas)


def _dft_tables(n):
    big = n * n
    k1 = jnp.arange(n, dtype=jnp.int32)
    ph1 = (2.0 * math.pi / n) * ((k1[:, None] * k1[None, :n // 2]) % n).astype(F32)
    f1 = jnp.stack([jnp.cos(ph1), -jnp.sin(ph1)], axis=1).reshape(2 * n, n // 2)
    f4 = f1.T / big
    idx = (k1[None, None, :] * (k1[:, None, None] + n * k1[None, :, None])) % big
    th = (2.0 * math.pi / big) * idx.astype(F32)
    gr, gi = jnp.cos(th), -jnp.sin(th)
    gfwd = jnp.concatenate([jnp.concatenate([gr, -gi], axis=2),
                            jnp.concatenate([gi, gr], axis=2)], axis=1)
    hr, hi = jnp.cos(th).transpose(0, 2, 1), jnp.sin(th).transpose(0, 2, 1)
    ginv = jnp.concatenate([jnp.concatenate([hr, -hi], axis=2),
                            jnp.concatenate([hi, hr], axis=2)], axis=1)
    return f1.astype(BF16), f4.astype(BF16), gfwd.astype(BF16), ginv.astype(BF16)


def _dft1_kernel(x_ref, f_ref, o_ref):
    o_ref[0] = jnp.dot(f_ref[...], x_ref[0].astype(BF16),
                       preferred_element_type=F32).astype(o_ref.dtype)


def _dft1(x, f1, n):
    bsz, seq, ch = x.shape
    cols = n * ch
    tn = min(8192, cols)
    return pl.pallas_call(
        _dft1_kernel,
        grid=(bsz, cols // tn),
        in_specs=[pl.BlockSpec((1, n // 2, tn), lambda b, j: (b, 0, j)),
                  pl.BlockSpec((2 * n, n // 2), lambda b, j: (0, 0))],
        out_specs=pl.BlockSpec((1, 2 * n, tn), lambda b, j: (b, 0, j)),
        out_shape=jax.ShapeDtypeStruct((bsz, 2 * n, cols), BF16),
        compiler_params=_cp("parallel", "parallel"),
        name="dft_stage1",
    )(x.reshape(bsz, n // 2, cols), f1)


def _filt_spec_kernel(af_ref, ab_ref, g_ref, hre_ref, him_ref):
    n = g_ref.shape[1] // 2
    for i in range(g_ref.shape[0]):
        xf = jnp.dot(g_ref[i], af_ref[0, i], preferred_element_type=F32)
        xb = jnp.dot(g_ref[i], ab_ref[0, i], preferred_element_type=F32)
        hre_ref[i] = xf[:n] + xb[:n]
        him_ref[i] = xf[n:] - xb[n:]


def _conv_spec_kernel(a_ref, gf_ref, gi_ref, hre_ref, him_ref, c_ref):
    n = gf_ref.shape[1] // 2
    for i in range(gf_ref.shape[0]):
        x = jnp.dot(gf_ref[i], a_ref[0, i], preferred_element_type=F32)
        xr, xi = x[:n], x[n:]
        hr, hi = hre_ref[i], him_ref[i]
        pr = xr * hr - xi * hi
        pi = xr * hi + xi * hr
        pc = jnp.concatenate([pr, pi], axis=0).astype(BF16)
        c_ref[0, i] = jnp.dot(gi_ref[i], pc, preferred_element_type=F32).astype(c_ref.dtype)


def _dft4_kernel(c_ref, f_ref, x0_ref, gv_ref, bias_ref, o_ref):
    y = jnp.dot(f_ref[...], c_ref[0], preferred_element_type=F32)
    gv = gv_ref[0]
    o_ref[0] = (x0_ref[0] * (y + gv * bias_ref[...])).astype(o_ref.dtype)


def _hyena(proj, tables, conv_w, conv_b, f_w1, f_b1, f_freq, f_w2, f_b2, f_w3, f_bias):
    bsz, seq, _ = proj.shape
    n = math.isqrt(2 * seq)
    assert n * n == 2 * seq and n % 2 == 0
    f1, f4, gfwd, ginv = tables
    ch = HY_W
    kb = min(8, n)
    x0c, gv = _hy_prep(proj, conv_w, conv_b)
    filt = _hy_filter(seq, f_w1, f_b1, f_freq, f_w2, f_b2, f_w3)

    a_f = _dft1(filt, f1, n).reshape(2, n, 2 * n, ch)
    adir = lambda bb: pl.BlockSpec((1, kb, 2 * n, ch), lambda k: (bb, k, 0, 0))
    gspec1 = pl.BlockSpec((kb, 2 * n, 2 * n), lambda k: (k, 0, 0))
    hspec1 = pl.BlockSpec((kb, n, ch), lambda k: (k, 0, 0))
    h_re, h_im = pl.pallas_call(
        _filt_spec_kernel,
        grid=(n // kb,),
        in_specs=[adir(0), adir(1), gspec1],
        out_specs=[hspec1, hspec1],
        out_shape=[jax.ShapeDtypeStruct((n, n, ch), F32)] * 2,
        compiler_params=_cp("parallel"),
        name="hyena_filter_spectrum",
    )(a_f, a_f, gfwd)

    a = _dft1(gv, f1, n).reshape(bsz, n, 2 * n, ch)
    aspec = pl.BlockSpec((1, kb, 2 * n, ch), lambda k, b: (b, k, 0, 0))
    gspec = pl.BlockSpec((kb, 2 * n, 2 * n), lambda k, b: (k, 0, 0))
    hspec = pl.BlockSpec((kb, n, ch), lambda k, b: (k, 0, 0))
    c = pl.pallas_call(
        _conv_spec_kernel,
        grid=(n // kb, bsz),
        in_specs=[aspec, gspec, gspec, hspec, hspec],
        out_specs=aspec,
        out_shape=jax.ShapeDtypeStruct((bsz, n, 2 * n, ch), BF16),
        compiler_params=_cp("parallel", "parallel"),
        name="hyena_conv_spectrum",
    )(a, gfwd, ginv, h_re, h_im).reshape(bsz, 2 * n, n * ch)

    cols = n * ch
    tn = min(8192, cols)
    bias = jnp.tile(f_bias.astype(F32), tn // ch).reshape(1, tn)
    rspec = pl.BlockSpec((1, n // 2, tn), lambda b, j: (b, 0, j))
    out = pl.pallas_call(
        _dft4_kernel,
        grid=(bsz, cols // tn),
        in_specs=[pl.BlockSpec((1, 2 * n, tn), lambda b, j: (b, 0, j)),
                  pl.BlockSpec((n // 2, 2 * n), lambda b, j: (0, 0)),
                  rspec, rspec,
                  pl.BlockSpec((1, tn), lambda b, j: (0, 0))],
        out_specs=rspec,
        out_shape=jax.ShapeDtypeStruct((bsz, n // 2, cols), BF16),
        compiler_params=_cp("parallel", "parallel"),
        name="dft_stage4",
    )(c, f4, x0c.reshape(bsz, n // 2, cols), gv.reshape(bsz, n // 2, cols), bias)
    return out.reshape(bsz, seq, ch)


def _outproj_kernel(om_ref, os_ref, oh_ref, x_ref, mod_ref, w_ref, g_ref, b_ref, o_ref):
    n_m, n_s = om_ref.shape[2], os_ref.shape[2]
    mixed = jnp.dot(om_ref[0], w_ref[0:n_m], preferred_element_type=F32)
    mixed += jnp.dot(os_ref[0], w_ref[n_m:n_m + n_s], preferred_element_type=F32)
    mixed += jnp.dot(oh_ref[0], w_ref[n_m + n_s:], preferred_element_type=F32)
    gate = mod_ref[0][2:3]
    o_ref[0] = _layer_norm(ALPHA * x_ref[0] + gate * mixed, g_ref[...], b_ref[...])


def _outproj(o_mla, o_s5, o_hy, x, mod, w, ln_g, ln_b):
    bsz, seq, d = x.shape
    tm = min(512, seq)
    row = lambda width: pl.BlockSpec((1, tm, width), lambda b, i: (b, i, 0))
    const = lambda shp: pl.BlockSpec(shp, lambda b, i: (0,) * len(shp))
    return pl.pallas_call(
        _outproj_kernel,
        grid=(bsz, seq // tm),
        in_specs=[row(o_mla.shape[2]), row(o_s5.shape[2]), row(o_hy.shape[2]), row(d),
                  pl.BlockSpec((1, 6, d), lambda b, i: (b, 0, 0)),
                  const(w.shape), const((1, d)), const((1, d))],
        out_specs=row(d),
        out_shape=jax.ShapeDtypeStruct((bsz, seq, d), F32),
        compiler_params=_cp("parallel", "parallel"),
        name="out_proj_ln",
    )(o_mla, o_s5, o_hy, x, mod, w, ln_g.reshape(1, -1), ln_b.reshape(1, -1))


def _ffn_kernel(xp_ref, x_ref, xn_ref, mod_ref, wg_ref, wu_ref, cw_ref, cb_ref, wd_ref,
                g_ref, b_ref, o_ref, u_scr, acc_scr):
    i, j = pl.program_id(1), pl.program_id(2)
    tm = x_ref.shape[1]
    halo = xp_ref.shape[1]

    @pl.when(j == 0)
    def _():
        m = mod_ref[0]
        sc, sh = 1.0 + m[4:5], m[3:4]
        u_scr[0:halo] = (xp_ref[0] * sc + sh).astype(BF16)
        u_scr[halo:halo + tm] = (x_ref[0] * sc + sh).astype(BF16)
        u_scr[halo + tm:] = (xn_ref[0] * sc + sh).astype(BF16)
        acc_scr[...] = jnp.zeros_like(acc_scr)

    u = u_scr[...]
    rows = tm + 2 * halo
    gx = jnp.dot(u, wg_ref[...], preferred_element_type=F32)
    r = lax.broadcasted_iota(jnp.int32, (tm, 1), 0)
    keep_prev = jnp.logical_or(r > 0, i > 0)
    keep_next = jnp.logical_or(r < tm - 1, i < pl.num_programs(1) - 1)
    g_prev = jnp.where(keep_prev, pltpu.roll(gx, 1, 0)[halo:halo + tm], 0.0)
    g_next = jnp.where(keep_next, pltpu.roll(gx, rows - 1, 0)[halo:halo + tm], 0.0)
    cw = cw_ref[...]
    conv = g_prev * cw[0:1] + gx[halo:halo + tm] * cw[1:2] + g_next * cw[2:3] + cb_ref[...]
    up = jnp.dot(u[halo:halo + tm], wu_ref[...], preferred_element_type=F32)
    h = (conv * jax.nn.sigmoid(conv) * up).astype(BF16)
    acc_scr[...] += jnp.dot(h, wd_ref[...], preferred_element_type=F32)

    @pl.when(j == pl.num_programs(2) - 1)
    def _():
        gate = mod_ref[0][5:6]
        o_ref[0] = _layer_norm(ALPHA * x_ref[0] + gate * acc_scr[...], g_ref[...], b_ref[...])


def _ffn(x, mod, wg, wu, conv_w, conv_b, wd, ln_g, ln_b):
    bsz, seq, d = x.shape
    ff = wg.shape[1]
    tm = min(512, seq)
    tn = 512
    halo = SUBLANE
    nhb = seq // halo
    per = tm // halo
    const = lambda shp: pl.BlockSpec(shp, lambda b, i, j: (0,) * len(shp))
    return pl.pallas_call(
        _ffn_kernel,
        grid=(bsz, seq // tm, ff // tn),
        in_specs=[pl.BlockSpec((1, halo, d), lambda b, i, j: (b, jnp.maximum(i * per - 1, 0), 0)),
                  pl.BlockSpec((1, tm, d), lambda b, i, j: (b, i, 0)),
                  pl.BlockSpec((1, halo, d), lambda b, i, j: (b, jnp.minimum((i + 1) * per, nhb - 1), 0)),
                  pl.BlockSpec((1, 6, d), lambda b, i, j: (b, 0, 0)),
                  pl.BlockSpec((d, tn), lambda b, i, j: (0, j)),
                  pl.BlockSpec((d, tn), lambda b, i, j: (0, j)),
                  pl.BlockSpec((3, tn), lambda b, i, j: (0, j)),
                  pl.BlockSpec((1, tn), lambda b, i, j: (0, j)),
                  pl.BlockSpec((tn, d), lambda b, i, j: (j, 0)),
                  const((1, d)), const((1, d))],
        out_specs=pl.BlockSpec((1, tm, d), lambda b, i, j: (b, i, 0)),
        out_shape=jax.ShapeDtypeStruct((bsz, seq, d), F32),
        scratch_shapes=[pltpu.VMEM((tm + 2 * halo, d), BF16), pltpu.VMEM((tm, d), F32)],
        compiler_params=_cp("parallel", "parallel", "arbitrary"),
        name="conv_ffn_ln",
    )(x, x, x, mod, wg, wu, conv_w.astype(F32), conv_b.reshape(1, -1).astype(F32), wd,
      ln_g.reshape(1, -1), ln_b.reshape(1, -1))


def _rot_half_cols(w):
    half = w.shape[-1] // 2
    return jnp.concatenate([-w[..., half:], w[..., :half]], axis=-1)


def _prep_w_in(w):
    q, kv, kr, s5, hy = jnp.split(w, (512, 768, 832, 1344), axis=-1)
    return jnp.concatenate([q, s5, hy, kv, kr, _rot_half_cols(kr)], axis=-1).astype(BF16)


def _prep_w_uq(w):
    w = w.reshape(w.shape[0], MLA_HEADS, MLA_QK)
    pe = w[..., MLA_NOPE:]
    return jnp.concatenate([w, _rot_half_cols(pe)], axis=-1).reshape(w.shape[0], -1).astype(BF16)


def kernel(x, c, positions, ada_w, ada_b, w_in, q_norm_g, kv_norm_g, w_uq, w_ukv, s5_a_re, s5_a_im, s5_log_step, s5_b_re, s5_b_im, s5_c_re, s5_c_im, s5_d, s5_glu_w, s5_glu_b, hy_conv_w, hy_conv_b, hy_f_w1, hy_f_b1, hy_f_freq, hy_f_w2, hy_f_b2, hy_f_w3, hy_f_bias, w_out, ln1_g, ln1_b, ffn_w_gate, ffn_w_up, ffn_conv_w, ffn_conv_b, ffn_w_down, ln2_g, ln2_b):
    bsz, seq, _ = x.shape
    depth = ada_w.shape[0]
    mods = _ada(c, ada_w, ada_b)
    tables = _dft_tables(math.isqrt(2 * seq))
    for l in range(depth):
        mod = mods[l]
        proj = _inproj(x, mod, _prep_w_in(w_in[l]))
        q, k, v = _mla_prep(proj, positions, q_norm_g[l], kv_norm_g[l],
                            _prep_w_uq(w_uq[l]), w_ukv[l].astype(BF16))
        o_mla = _attention(q, k, v)
        s5p = _s5_params(s5_a_re[l], s5_a_im[l], s5_log_step[l], s5_b_re[l], s5_b_im[l],
                         s5_c_re[l], s5_c_im[l], s5_d[l])
        o_s5 = _s5(proj, s5p, s5_glu_w[l], s5_glu_b[l])
        o_hy = _hyena(proj, tables, hy_conv_w[l], hy_conv_b[l], hy_f_w1[l], hy_f_b1[l],
                      hy_f_freq[l], hy_f_w2[l], hy_f_b2[l], hy_f_w3[l], hy_f_bias[l])
        x = _outproj(o_mla, o_s5, o_hy, x, mod, w_out[l].astype(BF16), ln1_g[l], ln1_b[l])
        x = _ffn(x, mod, ffn_w_gate[l].astype(BF16), ffn_w_up[l].astype(BF16), ffn_conv_w[l],
                 ffn_conv_b[l], ffn_w_down[l].astype(BF16), ln2_g[l], ln2_b[l])
    return x
```

```python
import functools
import math

import numpy as np
import jax
import jax.numpy as jnp
from jax import lax
from jax.experimental import pallas as pl
from jax.experimental.pallas import tpu as pltpu

F32 = jnp.float32
BF16 = jnp.bfloat16

MLA_HEADS = 8
MLA_NOPE = 128
MLA_ROPE = 64
MLA_V = 128
MLA_QK = MLA_NOPE + MLA_ROPE
MLA_Q_RANK = 512
MLA_KV_RANK = 256
ROPE_THETA = 10000.0
S5_W = 512
S5_GROUP = 16
S5_GROUPS = 32
S5_STATE = 64
S5_CHUNK = 16
S5_OCT = 8
HY_W = 512
HY_BANDS = 16
HY_DECAY_TARGET = 1e-2
HY_FAST_PCT = 0.3
HY_SLOW_PCT = 1.5
LN_EPS = 1e-5
RMS_EPS = 1e-6
FILTER_EPS = 1e-6
DEPTH = 2
ALPHA = (2 * DEPTH) ** 0.25

COL_Q = 0
COL_HY = 512
COL_KV = 2048
COL_KR = 2304
COL_S5 = 2432
IN_COLS_PAD = 2944

LANE = 128
SUBLANE = 8
VMEM_LIMIT = 56 * 1024 * 1024


def _cp(*sem):
    return pltpu.CompilerParams(dimension_semantics=sem, vmem_limit_bytes=VMEM_LIMIT)


def _layer_norm(y, g, b):
    mu = jnp.mean(y, axis=-1, keepdims=True)
    d = y - mu
    var = jnp.mean(d * d, axis=-1, keepdims=True)
    return d * lax.rsqrt(var + LN_EPS) * g + b


def _ada_kernel(c_ref, w_ref, b_ref, o_ref):
    c = c_ref[...]
    cond = c * jax.nn.sigmoid(c)
    o_ref[0] = jnp.dot(cond, w_ref[0], preferred_element_type=F32) + b_ref[0]


def _ada(c, ada_w, ada_b):
    bsz, d = c.shape
    depth, _, n = ada_w.shape
    tn = 1024
    cp = jnp.zeros((SUBLANE, d), F32).at[:bsz].set(c)
    out = pl.pallas_call(
        _ada_kernel,
        grid=(depth, n // tn),
        in_specs=[pl.BlockSpec((SUBLANE, d), lambda l, j: (0, 0)),
                  pl.BlockSpec((1, d, tn), lambda l, j: (l, 0, j)),
                  pl.BlockSpec((1, 1, tn), lambda l, j: (l, 0, j))],
        out_specs=pl.BlockSpec((1, SUBLANE, tn), lambda l, j: (l, 0, j)),
        out_shape=jax.ShapeDtypeStruct((depth, SUBLANE, n), F32),
        compiler_params=_cp("parallel", "parallel"),
        name="ada_mod",
    )(cp, ada_w, ada_b.reshape(depth, 1, n))
    return out[:, :bsz].reshape(depth, bsz, 6, d)


def _chunk_perm(chunks, to_time_major):
    q = S5_CHUNK
    nat = np.arange(chunks * q).reshape(chunks, q)
    tmaj = nat.T.reshape(-1)
    p = np.zeros((chunks * q, chunks * q), np.float32)
    if to_time_major:
        p[np.arange(chunks * q), tmaj] = 1.0
    else:
        p[tmaj, np.arange(chunks * q)] = 1.0
    return jnp.asarray(p, BF16)


def _inproj_kernel(x_ref, mod_ref, w_ref, perm_ref, o_ref, xs_ref):
    m = mod_ref[0]
    u = x_ref[0] * (1.0 + m[1:2]) + m[0:1]
    res = jnp.dot(u.astype(BF16), w_ref[...], preferred_element_type=F32)
    o_ref[0] = res[:, :COL_S5]
    us = jnp.dot(perm_ref[...], res[:, COL_S5:].astype(BF16), preferred_element_type=F32).astype(BF16)
    cn = xs_ref.shape[1]
    oct_w = S5_CHUNK * LANE
    for t in range(S5_CHUNK):
        for o in range(S5_W // LANE):
            xs_ref[0, :, o * oct_w + t * LANE:o * oct_w + (t + 1) * LANE] = (
                us[t * cn:(t + 1) * cn, o * LANE:(o + 1) * LANE])


def _inproj(x, mod, w):
    bsz, seq, d = x.shape
    n = w.shape[1]
    tm = min(256, seq)
    cn = tm // S5_CHUNK
    return pl.pallas_call(
        _inproj_kernel,
        grid=(bsz, seq // tm),
        in_specs=[pl.BlockSpec((1, tm, d), lambda b, i: (b, i, 0)),
                  pl.BlockSpec((1, 6, d), lambda b, i: (b, 0, 0)),
                  pl.BlockSpec((d, n), lambda b, i: (0, 0)),
                  pl.BlockSpec((tm, tm), lambda b, i: (0, 0))],
        out_specs=[pl.BlockSpec((1, tm, COL_S5), lambda b, i: (b, i, 0)),
                   pl.BlockSpec((1, cn, S5_CHUNK * S5_W), lambda b, i: (b, i, 0))],
        out_shape=[jax.ShapeDtypeStruct((bsz, seq, COL_S5), F32),
                   jax.ShapeDtypeStruct((bsz, seq // S5_CHUNK, S5_CHUNK * S5_W), BF16)],
        compiler_params=_cp("parallel", "parallel"),
        name="in_proj",
    )(x, mod, w, _chunk_perm(cn, True))


def _mla_prep_kernel(ql_ref, kvl_ref, kr_ref, pos_ref, invf_ref, qg_ref, kvg_ref,
                     wq_ref, wkv_ref, q_ref, k_ref, v_ref):
    scale = MLA_QK ** -0.5 * math.log2(math.e)
    ql = ql_ref[0]
    qn = ql * lax.rsqrt(jnp.mean(ql * ql, axis=-1, keepdims=True) + RMS_EPS) * qg_ref[...]
    kvl = kvl_ref[0]
    kvn = kvl * lax.rsqrt(jnp.mean(kvl * kvl, axis=-1, keepdims=True) + RMS_EPS) * kvg_ref[...]
    ang = pos_ref[0] * invf_ref[...]
    cos = jnp.cos(ang)
    sin = jnp.sin(ang)
    qa = jnp.dot(qn.astype(BF16), wq_ref[...], preferred_element_type=F32)
    kva = jnp.dot(kvn.astype(BF16), wkv_ref[...], preferred_element_type=F32)
    kr = kr_ref[0]
    kpe = (kr * cos + pltpu.roll(kr, MLA_ROPE, 1) * sin)[:, :MLA_ROPE].astype(BF16)
    for h in range(MLA_HEADS):
        c0 = 2 * LANE * h
        pr = qa[:, c0 + LANE:c0 + 2 * LANE]
        qpe = pr * cos + pltpu.roll(pr, MLA_ROPE, 1) * sin
        q_ref[0, h, :, 0:MLA_NOPE] = (qa[:, c0:c0 + LANE] * scale).astype(BF16)
        q_ref[0, h, :, MLA_NOPE:MLA_QK] = (qpe[:, :MLA_ROPE] * scale).astype(BF16)
        k_ref[0, h, :, 0:MLA_NOPE] = kva[:, c0:c0 + LANE].astype(BF16)
        k_ref[0, h, :, MLA_NOPE:MLA_QK] = kpe
        v_ref[0, h] = kva[:, c0 + LANE:c0 + 2 * LANE].astype(BF16)


def _mla_prep(proj, positions, q_g, kv_g, wq, wkv):
    bsz, seq, _ = proj.shape
    tm = min(512, seq)
    inv_freq = ROPE_THETA ** (-jnp.arange(0, MLA_ROPE, 2, dtype=F32) / MLA_ROPE)
    invf = jnp.tile(inv_freq, LANE // (MLA_ROPE // 2)).reshape(1, LANE)
    pos = positions.astype(F32).reshape(bsz, seq, 1)
    hshape = lambda w: jax.ShapeDtypeStruct((bsz, MLA_HEADS, seq, w), BF16)
    hspec = lambda w: pl.BlockSpec((1, MLA_HEADS, tm, w), lambda b, i: (b, 0, i, 0))
    const = lambda shp: pl.BlockSpec(shp, lambda b, i: (0,) * len(shp))
    return pl.pallas_call(
        _mla_prep_kernel,
        grid=(bsz, seq // tm),
        in_specs=[pl.BlockSpec((1, tm, MLA_Q_RANK), lambda b, i: (b, i, COL_Q // MLA_Q_RANK)),
                  pl.BlockSpec((1, tm, MLA_KV_RANK), lambda b, i: (b, i, COL_KV // MLA_KV_RANK)),
                  pl.BlockSpec((1, tm, LANE), lambda b, i: (b, i, COL_KR // LANE)),
                  pl.BlockSpec((1, tm, 1), lambda b, i: (b, i, 0)),
                  const((1, LANE)), const((1, MLA_Q_RANK)), const((1, MLA_KV_RANK)),
                  const(wq.shape), const(wkv.shape)],
        out_specs=[hspec(MLA_QK), hspec(MLA_QK), hspec(MLA_V)],
        out_shape=[hshape(MLA_QK), hshape(MLA_QK), hshape(MLA_V)],
        compiler_params=_cp("parallel", "parallel"),
        name="mla_prep",
    )(proj, proj, proj, pos, invf, q_g.reshape(1, -1), kv_g.reshape(1, -1), wq, wkv)


def _attn_kernel(q_ref, k_ref, v_ref, o_ref, *, tk):
    q = q_ref[0, 0]
    tq = q.shape[0]
    nk = k_ref.shape[2] // tk

    def body(j, carry):
        m, l, acc = carry
        off = pl.multiple_of(j * tk, tk)
        kj = k_ref[0, 0, pl.ds(off, tk), :]
        vj = v_ref[0, 0, pl.ds(off, tk), :]
        s = lax.dot_general(q, kj, (((1,), (1,)), ((), ())), preferred_element_type=F32)
        m_new = jnp.maximum(m, jnp.max(s, axis=-1, keepdims=True))
        a = jnp.exp2(m - m_new)
        p = jnp.exp2(s - m_new)
        l = a * l + jnp.sum(p, axis=-1, keepdims=True)
        acc = a * acc + jnp.dot(p.astype(BF16), vj, preferred_element_type=F32)
        return m_new, l, acc

    init = (jnp.full((tq, 1), jnp.finfo(F32).min, F32), jnp.zeros((tq, 1), F32),
            jnp.zeros((tq, MLA_V), F32))
    _, l, acc = lax.fori_loop(0, nk, body, init, unroll=True)
    o_ref[0] = (acc / l).astype(o_ref.dtype)


def _attention(q, k, v):
    bsz, nh, seq, _ = q.shape
    tq = min(512, seq)
    tk = min(2048, seq)
    return pl.pallas_call(
        functools.partial(_attn_kernel, tk=tk),
        grid=(bsz, nh, seq // tq),
        in_specs=[pl.BlockSpec((1, 1, tq, MLA_QK), lambda b, h, i: (b, h, i, 0)),
                  pl.BlockSpec((1, 1, seq, MLA_QK), lambda b, h, i: (b, h, 0, 0)),
                  pl.BlockSpec((1, 1, seq, MLA_V), lambda b, h, i: (b, h, 0, 0))],
        out_specs=pl.BlockSpec((1, tq, MLA_V), lambda b, h, i: (b, i, h)),
        out_shape=jax.ShapeDtypeStruct((bsz, seq, nh * MLA_V), BF16),
        compiler_params=_cp("parallel", "parallel", "parallel"),
        name="mla_attention",
    )(q, k, v)


def _s5_params(a_re, a_im, log_step, b_re, b_im, c_re, c_im, d):
    q = S5_CHUNK
    g, p, hh = S5_GROUPS, S5_STATE, S5_GROUP
    ks = jnp.arange(q + 1, dtype=F32)[:, None, None]
    out = {}
    lag_k = []
    for direction in range(2):
        ar, ai = a_re[direction].astype(F32), a_im[direction].astype(F32)
        step = jnp.exp(log_step[direction].astype(F32))[:, None]
        mag = jnp.exp(step * ar)
        abar_r = mag * jnp.cos(step * ai)
        abar_i = mag * jnp.sin(step * ai)
        den = ar * ar + ai * ai
        nr = abar_r - 1.0
        fr = (nr * ar + abar_i * ai) / den
        fi = (abar_i * ar - nr * ai) / den
        pmag = jnp.exp(ks * (step * ar)[None])
        pw_r = pmag * jnp.cos(ks * (step * ai)[None])
        pw_i = pmag * jnp.sin(ks * (step * ai)[None])
        br, bi = b_re[direction].astype(F32), b_im[direction].astype(F32)
        bf_r = fr[..., None] * br - fi[..., None] * bi
        bf_i = fr[..., None] * bi + fi[..., None] * br
        cr, ci = c_re[direction].astype(F32), c_im[direction].astype(F32)
        cp_r = cr[None] * pw_r[:, :, None, :] - ci[None] * pw_i[:, :, None, :]
        cp_i = cr[None] * pw_i[:, :, None, :] + ci[None] * pw_r[:, :, None, :]
        lag = (jnp.einsum('kghp,gpj->kghj', cp_r[:q], bf_r, precision='highest')
               - jnp.einsum('kghp,gpj->kghj', cp_i[:q], bf_i, precision='highest'))
        lag_k.append(lag)
        tpow = (q - 1 - jnp.arange(q)) if direction == 0 else jnp.arange(q)
        wr = pw_r[tpow][:, :, :, None] * bf_r[None] - pw_i[tpow][:, :, :, None] * bf_i[None]
        wi = pw_r[tpow][:, :, :, None] * bf_i[None] + pw_i[tpow][:, :, :, None] * bf_r[None]
        out[f'wst_r{direction}'] = wr.transpose(1, 0, 3, 2).reshape(g, q * hh, p)
        out[f'wst_i{direction}'] = wi.transpose(1, 0, 3, 2).reshape(g, q * hh, p)
        opow = (jnp.arange(q) + 1) if direction == 0 else (q - jnp.arange(q))
        out[f'cp_r{direction}'] = cp_r[opow].transpose(1, 3, 0, 2).reshape(g, p, q * hh)
        out[f'cp_i{direction}'] = (-cp_i[opow]).transpose(1, 3, 0, 2).reshape(g, p, q * hh)
        out[f'a_r{direction}'] = pw_r[q].reshape(1, g * p)
        out[f'a_i{direction}'] = pw_i[q].reshape(1, g * p)
    t_in = jnp.arange(q)[:, None]
    t_out = jnp.arange(q)[None, :]
    dlt = t_out - t_in
    kf = lag_k[0][jnp.clip(dlt, 0, q - 1)]
    kb = lag_k[1][jnp.clip(-dlt, 0, q - 1)]
    kf = jnp.where((dlt >= 0)[:, :, None, None, None], kf, 0.0)
    kb = jnp.where((dlt <= 0)[:, :, None, None, None], kb, 0.0)
    dmat = (d.astype(F32).reshape(g, hh)[None, None, :, :, None]
            * jnp.eye(hh, dtype=F32)[None, None, None]
            * jnp.eye(q, dtype=F32)[:, :, None, None, None])
    m = kf + kb + dmat
    out['m'] = m.transpose(2, 0, 4, 1, 3).reshape(g, q * hh, q * hh)
    return out


def _oct_tables(prm):
    q, hh, g, p, j = S5_CHUNK, S5_GROUP, S5_GROUPS, S5_STATE, S5_OCT
    no = g // j
    eye = jnp.eye(j, dtype=F32)
    feat = q * j * hh
    m = prm['m'].reshape(no, j, q, hh, q, hh)
    mm = jnp.einsum('ojaxby,jk->oajxbky', m, eye).reshape(no, feat, feat)

    def state(name):
        w = prm[name].reshape(no, j, q, hh, p)
        return jnp.einsum('ojaxp,jk->oajxkp', w, eye).reshape(no, feat, j * p)

    def outw(name):
        w = prm[name].reshape(no, j, p, q, hh)
        return jnp.einsum('ojpby,jk->ojpbky', w, eye).reshape(no, j * p, feat)

    wst = jnp.concatenate([state('wst_r0'), state('wst_i0'), state('wst_r1'), state('wst_i1')], axis=-1)
    cps = [outw(k).astype(BF16) for k in ('cp_r0', 'cp_i0', 'cp_r1', 'cp_i1')]
    return mm.astype(BF16), wst.astype(BF16), cps


def _s5_state_kernel(x_ref, w_ref, vfr_ref, vfi_ref, vbr_ref, vbi_ref):
    v = jnp.dot(x_ref[...], w_ref[0], preferred_element_type=F32)
    w = vfr_ref.shape[1]
    vfr_ref[...] = v[:, 0 * w:1 * w]
    vfi_ref[...] = v[:, 1 * w:2 * w]
    vbr_ref[...] = v[:, 2 * w:3 * w]
    vbi_ref[...] = v[:, 3 * w:4 * w]


def _s5_scan_kernel(vfr_ref, vfi_ref, vbr_ref, vbi_ref, afr_ref, afi_ref, abr_ref, abi_ref,
                    sfr_ref, sfi_ref, sbr_ref, sbi_ref):
    nc = vfr_ref.shape[1]
    afr, afi, abr, abi = afr_ref[...], afi_ref[...], abr_ref[...], abi_ref[...]

    def body(c, carry):
        fr, fi, br, bi = carry
        cb = nc - 1 - c
        sfr_ref[0, pl.ds(c, 1), :] = fr
        sfi_ref[0, pl.ds(c, 1), :] = fi
        sbr_ref[0, pl.ds(cb, 1), :] = br
        sbi_ref[0, pl.ds(cb, 1), :] = bi
        nfr = afr * fr - afi * fi + vfr_ref[0, pl.ds(c, 1), :]
        nfi = afr * fi + afi * fr + vfi_ref[0, pl.ds(c, 1), :]
        nbr = abr * br - abi * bi + vbr_ref[0, pl.ds(cb, 1), :]
        nbi = abr * bi + abi * br + vbi_ref[0, pl.ds(cb, 1), :]
        return nfr, nfi, nbr, nbi

    z = jnp.zeros((1, vfr_ref.shape[2]), F32)
    lax.fori_loop(0, nc, body, (z, z, z, z))


def _s5_out_kernel(x_ref, m_ref, sfr_ref, sfi_ref, sbr_ref, sbi_ref,
                   cfr_ref, cfi_ref, cbr_ref, cbi_ref, y_ref):
    y = jnp.dot(x_ref[...], m_ref[0], preferred_element_type=F32)
    y += jnp.dot(sfr_ref[...].astype(BF16), cfr_ref[0], preferred_element_type=F32)
    y += jnp.dot(sfi_ref[...].astype(BF16), cfi_ref[0], preferred_element_type=F32)
    y += jnp.dot(sbr_ref[...].astype(BF16), cbr_ref[0], preferred_element_type=F32)
    y += jnp.dot(sbi_ref[...].astype(BF16), cbi_ref[0], preferred_element_type=F32)
    y_ref[...] = y


def _s5_post_kernel(y_ref, w_ref, b_ref, perm_ref, o_ref):
    oct_w = S5_CHUNK * LANE
    y = jnp.concatenate(
        [jnp.concatenate([y_ref[:, o * oct_w + t * LANE:o * oct_w + (t + 1) * LANE]
                          for o in range(S5_W // LANE)], axis=1)
         for t in range(S5_CHUNK)], axis=0)
    y = jax.nn.gelu(y)
    z = jnp.dot(y.astype(BF16), w_ref[...], preferred_element_type=F32) + b_ref[...]
    o = (y * jax.nn.sigmoid(z)).astype(BF16)
    o_ref[...] = jnp.dot(perm_ref[...], o, preferred_element_type=F32).astype(o_ref.dtype)


def _s5(xs, prm, glu_w, glu_b):
    bsz, nc, _ = xs.shape
    q, g, p = S5_CHUNK, S5_GROUPS, S5_STATE
    seq = nc * q
    no = g // S5_OCT
    rows = bsz * nc
    feat = q * LANE
    sw = S5_OCT * p
    nst = g * p
    x = xs.reshape(rows, no * feat)
    mm, wst, cps = _oct_tables(prm)
    rt = min(512, rows)
    xspec = pl.BlockSpec((rt, feat), lambda o, i: (i, o))
    vspec = pl.BlockSpec((rt, sw), lambda o, i: (i, o))
    vshape = jax.ShapeDtypeStruct((rows, nst), F32)
    v4 = pl.pallas_call(
        _s5_state_kernel,
        grid=(no, rows // rt),
        in_specs=[xspec, pl.BlockSpec((1, feat, 4 * sw), lambda o, i: (o, 0, 0))],
        out_specs=[vspec] * 4,
        out_shape=[vshape] * 4,
        compiler_params=_cp("parallel", "parallel"),
        name="s5_state",
    )(x, wst)

    lt = min(512, nst)
    v4 = [v.reshape(bsz, nc, nst) for v in v4]
    sspec = pl.BlockSpec((1, nc, lt), lambda b, n: (b, 0, n))
    aspec = pl.BlockSpec((1, lt), lambda b, n: (0, n))
    s4 = pl.pallas_call(
        _s5_scan_kernel,
        grid=(bsz, nst // lt),
        in_specs=[sspec] * 4 + [aspec] * 4,
        out_specs=[sspec] * 4,
        out_shape=[jax.ShapeDtypeStruct((bsz, nc, nst), F32)] * 4,
        compiler_params=_cp("parallel", "parallel"),
        name="s5_scan",
    )(*v4, prm['a_r0'], prm['a_i0'], prm['a_r1'], prm['a_i1'])
    s4 = [s.reshape(rows, nst) for s in s4]

    ro = min(256, rows)
    xspec = pl.BlockSpec((ro, feat), lambda o, i: (i, o))
    vspec = pl.BlockSpec((ro, sw), lambda o, i: (i, o))
    cspec = pl.BlockSpec((1, sw, feat), lambda o, i: (o, 0, 0))
    y = pl.pallas_call(
        _s5_out_kernel,
        grid=(no, rows // ro),
        in_specs=[xspec, pl.BlockSpec((1, feat, feat), lambda o, i: (o, 0, 0))] + [vspec] * 4 + [cspec] * 4,
        out_specs=xspec,
        out_shape=jax.ShapeDtypeStruct((rows, no * feat), F32),
        compiler_params=_cp("parallel", "parallel"),
        name="s5_out",
    )(x, mm, *s4, *cps)

    cn = min(32, rows)
    const = lambda shp: pl.BlockSpec(shp, lambda i: (0,) * len(shp))
    out = pl.pallas_call(
        _s5_post_kernel,
        grid=(rows // cn,),
        in_specs=[pl.BlockSpec((cn, no * feat), lambda i: (i, 0)),
                  const((S5_W, S5_W)), const((1, S5_W)), const((cn * q, cn * q))],
        out_specs=pl.BlockSpec((cn * q, S5_W), lambda i: (i, 0)),
        out_shape=jax.ShapeDtypeStruct((rows * q, S5_W), BF16),
        compiler_params=_cp("parallel"),
        name="s5_post",
    )(y, glu_w.astype(BF16), glu_b.reshape(1, -1).astype(F32), _chunk_perm(cn, False))
    return out.reshape(bsz, seq, S5_W)


def _hy_prep_kernel(x0_ref, x1_ref, v_ref, w_ref, b_ref, x0c_ref, gv_ref):
    seq = x0_ref.shape[1]
    row = lax.broadcasted_iota(jnp.int32, (seq, LANE), 0)
    first = row == 0
    last = row == seq - 1

    def conv(ref, k):
        u = ref[0]
        w = w_ref[k]
        prev = jnp.where(first, 0.0, pltpu.roll(u, 1, 0))
        nxt = jnp.where(last, 0.0, pltpu.roll(u, seq - 1, 0))
        return prev * w[0:1] + u * w[1:2] + nxt * w[2:3] + b_ref[k]

    x0c_ref[0] = conv(x0_ref, 0)
    gv_ref[0] = conv(x1_ref, 1) * conv(v_ref, 2)


def _hy_prep(proj, conv_w, conv_b):
    bsz, seq, _ = proj.shape
    nt = HY_W // LANE
    base = COL_HY // LANE
    w = conv_w.astype(F32).reshape(3, 3, HY_W).transpose(1, 0, 2)
    b = conv_b.astype(F32).reshape(3, 1, HY_W)
    col = lambda part: pl.BlockSpec((1, seq, LANE), lambda bb, c: (bb, 0, base + part * nt + c))
    ospec = pl.BlockSpec((1, seq, LANE), lambda bb, c: (bb, 0, c))
    return pl.pallas_call(
        _hy_prep_kernel,
        grid=(bsz, nt),
        in_specs=[col(0), col(1), col(2),
                  pl.BlockSpec((3, 3, LANE), lambda bb, c: (0, 0, c)),
                  pl.BlockSpec((3, 1, LANE), lambda bb, c: (0, 0, c))],
        out_specs=[ospec, ospec],
        out_shape=[jax.ShapeDtypeStruct((bsz, seq, HY_W), F32)] * 2,
        compiler_params=_cp("parallel", "parallel"),
        name="hyena_prep",
    )(proj, proj, proj, w, b)


def _hy_filter_kernel(feat_ref, w1_ref, b1_ref, fq_ref, w2_ref, b2_ref, w3f_ref, w3b_ref,
                      dl_ref, o_ref):
    feat = feat_ref[...]
    fq = fq_ref[...]
    h = jnp.sin(fq[0:1] * (jnp.dot(feat, w1_ref[...], preferred_element_type=F32) + b1_ref[...]))
    h = jnp.sin(fq[1:2] * (jnp.dot(h, w2_ref[...], preferred_element_type=F32) + b2_ref[...]))
    win = jnp.exp(-feat[:, 0:1] * dl_ref[...])
    hf = jnp.dot(h, w3f_ref[...], preferred_element_type=F32) * win
    hb = jnp.dot(h, w3b_ref[...], preferred_element_type=F32) * win
    row = lax.broadcasted_iota(jnp.int32, hb.shape, 0)
    hb = jnp.where(row == 0, 0.0, hb)
    ss = jnp.sum(hf * hf, axis=0, keepdims=True) + jnp.sum(hb * hb, axis=0, keepdims=True)
    scale = lax.rsqrt(ss + FILTER_EPS)
    o_ref[0] = hf * scale
    o_ref[1] = hb * scale


def _hy_filter(seq, w1, b1, freq, w2, b2, w3):
    t01 = jnp.linspace(0.0, 1.0, seq, dtype=F32)[:, None]
    w = 2.0 * math.pi * jnp.arange(seq, dtype=F32)[:, None] / seq
    bands = jnp.linspace(1e-4, HY_BANDS - 1, HY_BANDS, dtype=F32)[None, :]
    fw = w * bands
    emb = 1 + 2 * HY_BANDS
    kpad = 64
    feat = jnp.concatenate([t01, jnp.cos(fw), -jnp.sin(fw), jnp.zeros((seq, kpad - emb), F32)], axis=-1)
    w1p = jnp.zeros((kpad, w1.shape[1]), F32).at[:emb].set(w1.astype(F32))
    deltas = jnp.abs(jnp.linspace(math.log(HY_DECAY_TARGET) / HY_SLOW_PCT,
                                  math.log(HY_DECAY_TARGET) / HY_FAST_PCT, HY_W, dtype=F32))[None, :]
    nt = HY_W // LANE
    hid = w2.shape[0]
    const = lambda shp: pl.BlockSpec(shp, lambda c: (0,) * len(shp))
    return pl.pallas_call(
        _hy_filter_kernel,
        grid=(nt,),
        in_specs=[const((seq, kpad)), const((kpad, hid)), const((1, hid)), const((2, hid)),
                  const((hid, hid)), const((1, hid)),
                  pl.BlockSpec((hid, LANE), lambda c: (0, c)),
                  pl.BlockSpec((hid, LANE), lambda c: (0, nt + c)),
                  pl.BlockSpec((1, LANE), lambda c: (0, c))],
        out_specs=pl.BlockSpec((2, seq, LANE), lambda c: (0, 0, c)),
        out_shape=jax.ShapeDtypeStruct((2, seq, HY_W), F32),
        compiler_params=_cp("parallel"),
        name="hyena_filter",
    )(feat, w1p, b1.reshape(1, -1).astype(F32), freq.astype(F32), w2.astype(F32),
      b2.reshape(1, -1).astype(F32), w3.astype(F32), w3.astype(F32), deltas)


def _dft_tables(n):
    big = n * n
    k1 = jnp.arange(n, dtype=jnp.int32)
    ph1 = (2.0 * math.pi / n) * ((k1[:, None] * k1[None, :n // 2]) % n).astype(F32)
    f1 = jnp.stack([jnp.cos(ph1), -jnp.sin(ph1)], axis=1).reshape(2 * n, n // 2)
    f4 = f1.T / big
    eye = jnp.eye(SUBLANE, dtype=F32)
    f1 = jnp.einsum('kn,rs->krns', f1, eye).reshape(2 * n * SUBLANE, (n // 2) * SUBLANE)
    f4 = jnp.einsum('nk,rs->nrks', f4, eye).reshape((n // 2) * SUBLANE, 2 * n * SUBLANE)
    idx = (k1[None, None, :] * (k1[:, None, None] + n * k1[None, :, None])) % big
    th = (2.0 * math.pi / big) * idx.astype(F32)
    gr, gi = jnp.cos(th), -jnp.sin(th)
    gfwd = jnp.concatenate([jnp.concatenate([gr, -gi], axis=2),
                            jnp.concatenate([gi, gr], axis=2)], axis=1)
    hr, hi = jnp.cos(th).transpose(0, 2, 1), jnp.sin(th).transpose(0, 2, 1)
    ginv = jnp.concatenate([jnp.concatenate([hr, -hi], axis=2),
                            jnp.concatenate([hi, hr], axis=2)], axis=1)
    return f1.astype(BF16), f4.astype(BF16), gfwd.astype(BF16), ginv.astype(BF16)


DFT_ROWS = 2 * SUBLANE


def _kron_dot(f_ref, x3):
    k, _, ch = x3.shape
    halves = []
    for h in range(DFT_ROWS // SUBLANE):
        xh = x3[:, h * SUBLANE:(h + 1) * SUBLANE, :].reshape(k * SUBLANE, ch).astype(BF16)
        yh = jnp.dot(f_ref[...], xh, preferred_element_type=F32)
        halves.append(yh.reshape(-1, SUBLANE, ch))
    return jnp.concatenate(halves, axis=1)


def _dft1_kernel(x_ref, f_ref, o_ref):
    o_ref[0, :, 0] = _kron_dot(f_ref, x_ref[0, :, 0]).astype(o_ref.dtype)


def _dft1(x, f1, n):
    bsz, seq, ch = x.shape
    nb = n // DFT_ROWS
    return pl.pallas_call(
        _dft1_kernel,
        grid=(bsz, nb),
        in_specs=[pl.BlockSpec((1, n // 2, 1, DFT_ROWS, ch), lambda b, j: (b, 0, j, 0, 0)),
                  pl.BlockSpec(f1.shape, lambda b, j: (0, 0))],
        out_specs=pl.BlockSpec((1, 2 * n, 1, DFT_ROWS, ch), lambda b, j: (b, 0, j, 0, 0)),
        out_shape=jax.ShapeDtypeStruct((bsz, 2 * n, nb, DFT_ROWS, ch), BF16),
        compiler_params=_cp("parallel", "parallel"),
        name="dft_stage1",
    )(x.reshape(bsz, n // 2, nb, DFT_ROWS, ch), f1)


def _filt_spec_kernel(af_ref, ab_ref, g_ref, hre_ref, him_ref):
    n = g_ref.shape[1] // 2
    for i in range(g_ref.shape[0]):
        xf = jnp.dot(g_ref[i], af_ref[0, i], preferred_element_type=F32)
        xb = jnp.dot(g_ref[i], ab_ref[0, i], preferred_element_type=F32)
        hre_ref[i] = xf[:n] + xb[:n]
        him_ref[i] = xf[n:] - xb[n:]


def _conv_spec_kernel(a_ref, gf_ref, gi_ref, hre_ref, him_ref, c_ref):
    n = gf_ref.shape[1] // 2
    for i in range(gf_ref.shape[0]):
        x = jnp.dot(gf_ref[i], a_ref[0, i], preferred_element_type=F32)
        xr, xi = x[:n], x[n:]
        hr, hi = hre_ref[i], him_ref[i]
        pr = xr * hr - xi * hi
        pi = xr * hi + xi * hr
        pc = jnp.concatenate([pr, pi], axis=0).astype(BF16)
        c_ref[0, i] = jnp.dot(gi_ref[i], pc, preferred_element_type=F32).astype(c_ref.dtype)


def _dft4_kernel(c_ref, f_ref, x0_ref, gv_ref, bias_ref, o_ref):
    y = _kron_dot(f_ref, c_ref[0, :, 0].astype(F32))
    o_ref[0, :, 0] = (x0_ref[0, :, 0] * (y + gv_ref[0, :, 0] * bias_ref[...])).astype(o_ref.dtype)


def _hyena(proj, tables, conv_w, conv_b, f_w1, f_b1, f_freq, f_w2, f_b2, f_w3, f_bias):
    bsz, seq, _ = proj.shape
    n = math.isqrt(2 * seq)
    assert n * n == 2 * seq and n % 2 == 0
    f1, f4, gfwd, ginv = tables
    ch = HY_W
    kb = min(8, n)
    x0c, gv = _hy_prep(proj, conv_w, conv_b)
    filt = _hy_filter(seq, f_w1, f_b1, f_freq, f_w2, f_b2, f_w3)

    a_f = _dft1(filt, f1, n).reshape(2, n, 2 * n, ch)
    adir = lambda bb: pl.BlockSpec((1, kb, 2 * n, ch), lambda k: (bb, k, 0, 0))
    gspec1 = pl.BlockSpec((kb, 2 * n, 2 * n), lambda k: (k, 0, 0))
    hspec1 = pl.BlockSpec((kb, n, ch), lambda k: (k, 0, 0))
    h_re, h_im = pl.pallas_call(
        _filt_spec_kernel,
        grid=(n // kb,),
        in_specs=[adir(0), adir(1), gspec1],
        out_specs=[hspec1, hspec1],
        out_shape=[jax.ShapeDtypeStruct((n, n, ch), F32)] * 2,
        compiler_params=_cp("parallel"),
        name="hyena_filter_spectrum",
    )(a_f, a_f, gfwd)

    a = _dft1(gv, f1, n).reshape(bsz, n, 2 * n, ch)
    aspec = pl.BlockSpec((1, kb, 2 * n, ch), lambda k, b: (b, k, 0, 0))
    gspec = pl.BlockSpec((kb, 2 * n, 2 * n), lambda k, b: (k, 0, 0))
    hspec = pl.BlockSpec((kb, n, ch), lambda k, b: (k, 0, 0))
    c = pl.pallas_call(
        _conv_spec_kernel,
        grid=(n // kb, bsz),
        in_specs=[aspec, gspec, gspec, hspec, hspec],
        out_specs=aspec,
        out_shape=jax.ShapeDtypeStruct((bsz, n, 2 * n, ch), BF16),
        compiler_params=_cp("parallel", "parallel"),
        name="hyena_conv_spectrum",
    )(a, gfwd, ginv, h_re, h_im).reshape(bsz, 2 * n, n // DFT_ROWS, DFT_ROWS, ch)

    nb = n // DFT_ROWS
    nat = lambda arr: arr.reshape(bsz, n // 2, nb, DFT_ROWS, ch)
    rspec = pl.BlockSpec((1, n // 2, 1, DFT_ROWS, ch), lambda b, j: (b, 0, j, 0, 0))
    out = pl.pallas_call(
        _dft4_kernel,
        grid=(bsz, nb),
        in_specs=[pl.BlockSpec((1, 2 * n, 1, DFT_ROWS, ch), lambda b, j: (b, 0, j, 0, 0)),
                  pl.BlockSpec(f4.shape, lambda b, j: (0, 0)),
                  rspec, rspec,
                  pl.BlockSpec((1, ch), lambda b, j: (0, 0))],
        out_specs=rspec,
        out_shape=jax.ShapeDtypeStruct((bsz, n // 2, nb, DFT_ROWS, ch), BF16),
        compiler_params=_cp("parallel", "parallel"),
        name="dft_stage4",
    )(c, f4, nat(x0c), nat(gv), f_bias.astype(F32).reshape(1, ch))
    return out.reshape(bsz, seq, ch)


def _outproj_kernel(om_ref, os_ref, oh_ref, x_ref, mod_ref, w_ref, g_ref, b_ref, o_ref):
    n_m, n_s = om_ref.shape[2], os_ref.shape[2]
    mixed = jnp.dot(om_ref[0], w_ref[0:n_m], preferred_element_type=F32)
    mixed += jnp.dot(os_ref[0], w_ref[n_m:n_m + n_s], preferred_element_type=F32)
    mixed += jnp.dot(oh_ref[0], w_ref[n_m + n_s:], preferred_element_type=F32)
    gate = mod_ref[0][2:3]
    o_ref[0] = _layer_norm(ALPHA * x_ref[0] + gate * mixed, g_ref[...], b_ref[...])


def _outproj(o_mla, o_s5, o_hy, x, mod, w, ln_g, ln_b):
    bsz, seq, d = x.shape
    tm = min(512, seq)
    row = lambda width: pl.BlockSpec((1, tm, width), lambda b, i: (b, i, 0))
    const = lambda shp: pl.BlockSpec(shp, lambda b, i: (0,) * len(shp))
    return pl.pallas_call(
        _outproj_kernel,
        grid=(bsz, seq // tm),
        in_specs=[row(o_mla.shape[2]), row(o_s5.shape[2]), row(o_hy.shape[2]), row(d),
                  pl.BlockSpec((1, 6, d), lambda b, i: (b, 0, 0)),
                  const(w.shape), const((1, d)), const((1, d))],
        out_specs=row(d),
        out_shape=jax.ShapeDtypeStruct((bsz, seq, d), F32),
        compiler_params=_cp("parallel", "parallel"),
        name="out_proj_ln",
    )(o_mla, o_s5, o_hy, x, mod, w, ln_g.reshape(1, -1), ln_b.reshape(1, -1))


def _ffn_kernel(xp_ref, x_ref, xn_ref, mod_ref, wg_ref, wu_ref, cw_ref, cb_ref, wd_ref,
                g_ref, b_ref, o_ref, u_scr, acc_scr):
    i, j = pl.program_id(1), pl.program_id(2)
    tm = x_ref.shape[1]
    halo = xp_ref.shape[1]

    @pl.when(j == 0)
    def _():
        m = mod_ref[0]
        sc, sh = 1.0 + m[4:5], m[3:4]
        u_scr[0:halo] = (xp_ref[0] * sc + sh).astype(BF16)
        u_scr[halo:halo + tm] = (x_ref[0] * sc + sh).astype(BF16)
        u_scr[halo + tm:] = (xn_ref[0] * sc + sh).astype(BF16)
        acc_scr[...] = jnp.zeros_like(acc_scr)

    u = u_scr[...]
    rows = tm + 2 * halo
    gx = jnp.dot(u, wg_ref[...], preferred_element_type=F32)
    r = lax.broadcasted_iota(jnp.int32, (tm, 1), 0)
    keep_prev = jnp.logical_or(r > 0, i > 0)
    keep_next = jnp.logical_or(r < tm - 1, i < pl.num_programs(1) - 1)
    g_prev = jnp.where(keep_prev, pltpu.roll(gx, 1, 0)[halo:halo + tm], 0.0)
    g_next = jnp.where(keep_next, pltpu.roll(gx, rows - 1, 0)[halo:halo + tm], 0.0)
    cw = cw_ref[...]
    conv = g_prev * cw[0:1] + gx[halo:halo + tm] * cw[1:2] + g_next * cw[2:3] + cb_ref[...]
    up = jnp.dot(u[halo:halo + tm], wu_ref[...], preferred_element_type=F32)
    h = (conv * jax.nn.sigmoid(conv) * up).astype(BF16)
    acc_scr[...] += jnp.dot(h, wd_ref[...], preferred_element_type=F32)

    @pl.when(j == pl.num_programs(2) - 1)
    def _():
        gate = mod_ref[0][5:6]
        o_ref[0] = _layer_norm(ALPHA * x_ref[0] + gate * acc_scr[...], g_ref[...], b_ref[...])


def _ffn(x, mod, wg, wu, conv_w, conv_b, wd, ln_g, ln_b):
    bsz, seq, d = x.shape
    ff = wg.shape[1]
    tm = min(512, seq)
    tn = 512
    halo = SUBLANE
    nhb = seq // halo
    per = tm // halo
    const = lambda shp: pl.BlockSpec(shp, lambda b, i, j: (0,) * len(shp))
    return pl.pallas_call(
        _ffn_kernel,
        grid=(bsz, seq // tm, ff // tn),
        in_specs=[pl.BlockSpec((1, halo, d), lambda b, i, j: (b, jnp.maximum(i * per - 1, 0), 0)),
                  pl.BlockSpec((1, tm, d), lambda b, i, j: (b, i, 0)),
                  pl.BlockSpec((1, halo, d), lambda b, i, j: (b, jnp.minimum((i + 1) * per, nhb - 1), 0)),
                  pl.BlockSpec((1, 6, d), lambda b, i, j: (b, 0, 0)),
                  pl.BlockSpec((d, tn), lambda b, i, j: (0, j)),
                  pl.BlockSpec((d, tn), lambda b, i, j: (0, j)),
                  pl.BlockSpec((3, tn), lambda b, i, j: (0, j)),
                  pl.BlockSpec((1, tn), lambda b, i, j: (0, j)),
                  pl.BlockSpec((tn, d), lambda b, i, j: (j, 0)),
                  const((1, d)), const((1, d))],
        out_specs=pl.BlockSpec((1, tm, d), lambda b, i, j: (b, i, 0)),
        out_shape=jax.ShapeDtypeStruct((bsz, seq, d), F32),
        scratch_shapes=[pltpu.VMEM((tm + 2 * halo, d), BF16), pltpu.VMEM((tm, d), F32)],
        compiler_params=_cp("parallel", "parallel", "arbitrary"),
        name="conv_ffn_ln",
    )(x, x, x, mod, wg, wu, conv_w.astype(F32), conv_b.reshape(1, -1).astype(F32), wd,
      ln_g.reshape(1, -1), ln_b.reshape(1, -1))


def _rot_half_cols(w):
    half = w.shape[-1] // 2
    return jnp.concatenate([-w[..., half:], w[..., :half]], axis=-1)


def _prep_w_in(w):
    q, kv, kr, s5, hy = jnp.split(w, (512, 768, 832, 1344), axis=-1)
    return jnp.concatenate([q, hy, kv, kr, _rot_half_cols(kr), s5], axis=-1).astype(BF16)


def _prep_w_uq(w):
    w = w.reshape(w.shape[0], MLA_HEADS, MLA_QK)
    pe = w[..., MLA_NOPE:]
    return jnp.concatenate([w, _rot_half_cols(pe)], axis=-1).reshape(w.shape[0], -1).astype(BF16)


def kernel(x, c, positions, ada_w, ada_b, w_in, q_norm_g, kv_norm_g, w_uq, w_ukv, s5_a_re, s5_a_im, s5_log_step, s5_b_re, s5_b_im, s5_c_re, s5_c_im, s5_d, s5_glu_w, s5_glu_b, hy_conv_w, hy_conv_b, hy_f_w1, hy_f_b1, hy_f_freq, hy_f_w2, hy_f_b2, hy_f_w3, hy_f_bias, w_out, ln1_g, ln1_b, ffn_w_gate, ffn_w_up, ffn_conv_w, ffn_conv_b, ffn_w_down, ln2_g, ln2_b):
    bsz, seq, _ = x.shape
    depth = ada_w.shape[0]
    mods = _ada(c, ada_w, ada_b)
    tables = _dft_tables(math.isqrt(2 * seq))
    for l in range(depth):
        mod = mods[l]
        proj, xs5 = _inproj(x, mod, _prep_w_in(w_in[l]))
        q, k, v = _mla_prep(proj, positions, q_norm_g[l], kv_norm_g[l],
                            _prep_w_uq(w_uq[l]), w_ukv[l].astype(BF16))
        o_mla = _attention(q, k, v)
        s5p = _s5_params(s5_a_re[l], s5_a_im[l], s5_log_step[l], s5_b_re[l], s5_b_im[l],
                         s5_c_re[l], s5_c_im[l], s5_d[l])
        o_s5 = _s5(xs5, s5p, s5_glu_w[l], s5_glu_b[l])
        o_hy = _hyena(proj, tables, hy_conv_w[l], hy_conv_b[l], hy_f_w1[l], hy_f_b1[l],
                      hy_f_freq[l], hy_f_w2[l], hy_f_b2[l], hy_f_w3[l], hy_f_bias[l])
        x = _outproj(o_mla, o_s5, o_hy, x, mod, w_out[l].astype(BF16), ln1_g[l], ln1_b[l])
        x = _ffn(x, mod, ffn_w_gate[l].astype(BF16), ffn_w_up[l].astype(BF16), ffn_conv_w[l],
                 ffn_conv_b[l], ffn_w_down[l].astype(BF16), ln2_g[l], ln2_b[l])
    return x
```

```python
import functools
import math

import numpy as np
import jax
import jax.numpy as jnp
from jax import lax
from jax.experimental import pallas as pl
from jax.experimental.pallas import tpu as pltpu

F32 = jnp.float32
BF16 = jnp.bfloat16

MLA_HEADS = 8
MLA_NOPE = 128
MLA_ROPE = 64
MLA_V = 128
MLA_QK = MLA_NOPE + MLA_ROPE
MLA_Q_RANK = 512
MLA_KV_RANK = 256
ROPE_THETA = 10000.0
S5_W = 512
S5_GROUP = 16
S5_GROUPS = 32
S5_STATE = 64
S5_CHUNK = 16
S5_OCT = 8
HY_W = 512
HY_BANDS = 16
HY_DECAY_TARGET = 1e-2
HY_FAST_PCT = 0.3
HY_SLOW_PCT = 1.5
LN_EPS = 1e-5
RMS_EPS = 1e-6
FILTER_EPS = 1e-6
DEPTH = 2
ALPHA = (2 * DEPTH) ** 0.25

COL_Q = 0
COL_HY = 512
COL_KV = 2048
COL_KR = 2304
COL_S5 = 2432
IN_COLS_PAD = 2944

LANE = 128
SUBLANE = 8
VMEM_LIMIT = 56 * 1024 * 1024


def _cp(*sem):
    return pltpu.CompilerParams(dimension_semantics=sem, vmem_limit_bytes=VMEM_LIMIT)


def _layer_norm(y, g, b):
    mu = jnp.mean(y, axis=-1, keepdims=True)
    d = y - mu
    var = jnp.mean(d * d, axis=-1, keepdims=True)
    return d * lax.rsqrt(var + LN_EPS) * g + b


def _ada_kernel(c_ref, w_ref, b_ref, o_ref):
    c = c_ref[...]
    cond = c * jax.nn.sigmoid(c)
    o_ref[0] = jnp.dot(cond, w_ref[0], preferred_element_type=F32) + b_ref[0]


def _ada(c, ada_w, ada_b):
    bsz, d = c.shape
    depth, _, n = ada_w.shape
    tn = 1024
    cp = jnp.zeros((SUBLANE, d), F32).at[:bsz].set(c)
    out = pl.pallas_call(
        _ada_kernel,
        grid=(depth, n // tn),
        in_specs=[pl.BlockSpec((SUBLANE, d), lambda l, j: (0, 0)),
                  pl.BlockSpec((1, d, tn), lambda l, j: (l, 0, j)),
                  pl.BlockSpec((1, 1, tn), lambda l, j: (l, 0, j))],
        out_specs=pl.BlockSpec((1, SUBLANE, tn), lambda l, j: (l, 0, j)),
        out_shape=jax.ShapeDtypeStruct((depth, SUBLANE, n), F32),
        compiler_params=_cp("parallel", "parallel"),
        name="ada_mod",
    )(cp, ada_w, ada_b.reshape(depth, 1, n))
    return out[:, :bsz].reshape(depth, bsz, 6, d)


def _chunk_perm(chunks, to_time_major):
    q = S5_CHUNK
    nat = np.arange(chunks * q).reshape(chunks, q)
    tmaj = nat.T.reshape(-1)
    p = np.zeros((chunks * q, chunks * q), np.float32)
    if to_time_major:
        p[np.arange(chunks * q), tmaj] = 1.0
    else:
        p[tmaj, np.arange(chunks * q)] = 1.0
    return jnp.asarray(p, BF16)


def _inproj_kernel(x_ref, mod_ref, w_ref, perm_ref, o_ref, xs_ref):
    m = mod_ref[0]
    u = x_ref[0] * (1.0 + m[1:2]) + m[0:1]
    res = jnp.dot(u.astype(BF16), w_ref[0], preferred_element_type=F32)
    o_ref[0] = res[:, :COL_S5]
    us = jnp.dot(perm_ref[...], res[:, COL_S5:].astype(BF16), preferred_element_type=F32).astype(BF16)
    cn = xs_ref.shape[1]
    oct_w = S5_CHUNK * LANE
    for t in range(S5_CHUNK):
        for o in range(S5_W // LANE):
            xs_ref[0, :, o * oct_w + t * LANE:o * oct_w + (t + 1) * LANE] = (
                us[t * cn:(t + 1) * cn, o * LANE:(o + 1) * LANE])


def _inproj(x, mod, w, l):
    bsz, seq, d = x.shape
    n = w.shape[2]
    tm = min(256, seq)
    cn = tm // S5_CHUNK
    return pl.pallas_call(
        _inproj_kernel,
        grid=(bsz, seq // tm),
        in_specs=[pl.BlockSpec((1, tm, d), lambda b, i: (b, i, 0)),
                  pl.BlockSpec((1, 6, d), lambda b, i: (b, 0, 0)),
                  pl.BlockSpec((1, d, n), lambda b, i: (l, 0, 0)),
                  pl.BlockSpec((tm, tm), lambda b, i: (0, 0))],
        out_specs=[pl.BlockSpec((1, tm, COL_S5), lambda b, i: (b, i, 0)),
                   pl.BlockSpec((1, cn, S5_CHUNK * S5_W), lambda b, i: (b, i, 0))],
        out_shape=[jax.ShapeDtypeStruct((bsz, seq, COL_S5), F32),
                   jax.ShapeDtypeStruct((bsz, seq // S5_CHUNK, S5_CHUNK * S5_W), BF16)],
        compiler_params=_cp("parallel", "parallel"),
        name="in_proj",
    )(x, mod, w, _chunk_perm(cn, True))


def _mla_prep_kernel(ql_ref, kvl_ref, kr_ref, pos_ref, invf_ref, qg_ref, kvg_ref,
                     wq_ref, wkv_ref, q_ref, k_ref, v_ref):
    scale = MLA_QK ** -0.5 * math.log2(math.e)
    ql = ql_ref[0]
    qn = ql * lax.rsqrt(jnp.mean(ql * ql, axis=-1, keepdims=True) + RMS_EPS) * qg_ref[...]
    kvl = kvl_ref[0]
    kvn = kvl * lax.rsqrt(jnp.mean(kvl * kvl, axis=-1, keepdims=True) + RMS_EPS) * kvg_ref[...]
    ang = pos_ref[0] * invf_ref[...]
    cos = jnp.cos(ang)
    sin = jnp.sin(ang)
    qa = jnp.dot(qn.astype(BF16), wq_ref[...], preferred_element_type=F32)
    kva = jnp.dot(kvn.astype(BF16), wkv_ref[...], preferred_element_type=F32)
    kr = kr_ref[0]
    kpe = (kr * cos + pltpu.roll(kr, MLA_ROPE, 1) * sin)[:, :MLA_ROPE].astype(BF16)
    for h in range(MLA_HEADS):
        c0 = 2 * LANE * h
        pr = qa[:, c0 + LANE:c0 + 2 * LANE]
        qpe = pr * cos + pltpu.roll(pr, MLA_ROPE, 1) * sin
        q_ref[0, h, :, 0:MLA_NOPE] = (qa[:, c0:c0 + LANE] * scale).astype(BF16)
        q_ref[0, h, :, MLA_NOPE:MLA_QK] = (qpe[:, :MLA_ROPE] * scale).astype(BF16)
        k_ref[0, h, :, 0:MLA_NOPE] = kva[:, c0:c0 + LANE].astype(BF16)
        k_ref[0, h, :, MLA_NOPE:MLA_QK] = kpe
        v_ref[0, h] = kva[:, c0 + LANE:c0 + 2 * LANE].astype(BF16)


def _mla_prep(proj, positions, q_g, kv_g, wq, wkv):
    bsz, seq, _ = proj.shape
    tm = min(512, seq)
    inv_freq = ROPE_THETA ** (-jnp.arange(0, MLA_ROPE, 2, dtype=F32) / MLA_ROPE)
    invf = jnp.tile(inv_freq, LANE // (MLA_ROPE // 2)).reshape(1, LANE)
    pos = positions.astype(F32).reshape(bsz, seq, 1)
    hshape = lambda w: jax.ShapeDtypeStruct((bsz, MLA_HEADS, seq, w), BF16)
    hspec = lambda w: pl.BlockSpec((1, MLA_HEADS, tm, w), lambda b, i: (b, 0, i, 0))
    const = lambda shp: pl.BlockSpec(shp, lambda b, i: (0,) * len(shp))
    return pl.pallas_call(
        _mla_prep_kernel,
        grid=(bsz, seq // tm),
        in_specs=[pl.BlockSpec((1, tm, MLA_Q_RANK), lambda b, i: (b, i, COL_Q // MLA_Q_RANK)),
                  pl.BlockSpec((1, tm, MLA_KV_RANK), lambda b, i: (b, i, COL_KV // MLA_KV_RANK)),
                  pl.BlockSpec((1, tm, LANE), lambda b, i: (b, i, COL_KR // LANE)),
                  pl.BlockSpec((1, tm, 1), lambda b, i: (b, i, 0)),
                  const((1, LANE)), const((1, MLA_Q_RANK)), const((1, MLA_KV_RANK)),
                  const(wq.shape), const(wkv.shape)],
        out_specs=[hspec(MLA_QK), hspec(MLA_QK), hspec(MLA_V)],
        out_shape=[hshape(MLA_QK), hshape(MLA_QK), hshape(MLA_V)],
        compiler_params=_cp("parallel", "parallel"),
        name="mla_prep",
    )(proj, proj, proj, pos, invf, q_g.reshape(1, -1), kv_g.reshape(1, -1), wq, wkv)


def _attn_kernel(q_ref, k_ref, v_ref, o_ref, *, tk):
    q = q_ref[0, 0]
    tq = q.shape[0]
    nk = k_ref.shape[2] // tk

    def body(j, carry):
        m, l, acc = carry
        off = pl.multiple_of(j * tk, tk)
        kj = k_ref[0, 0, pl.ds(off, tk), :]
        vj = v_ref[0, 0, pl.ds(off, tk), :]
        s = lax.dot_general(q, kj, (((1,), (1,)), ((), ())), preferred_element_type=F32)
        m_new = jnp.maximum(m, jnp.max(s, axis=-1, keepdims=True))
        a = jnp.exp2(m - m_new)
        p = jnp.exp2(s - m_new)
        l = a * l + jnp.sum(p, axis=-1, keepdims=True)
        acc = a * acc + jnp.dot(p.astype(BF16), vj, preferred_element_type=F32)
        return m_new, l, acc

    init = (jnp.full((tq, 1), jnp.finfo(F32).min, F32), jnp.zeros((tq, 1), F32),
            jnp.zeros((tq, MLA_V), F32))
    _, l, acc = lax.fori_loop(0, nk, body, init, unroll=True)
    o_ref[0] = (acc / l).astype(o_ref.dtype)


def _attention(q, k, v):
    bsz, nh, seq, _ = q.shape
    tq = min(512, seq)
    tk = min(2048, seq)
    return pl.pallas_call(
        functools.partial(_attn_kernel, tk=tk),
        grid=(bsz, nh, seq // tq),
        in_specs=[pl.BlockSpec((1, 1, tq, MLA_QK), lambda b, h, i: (b, h, i, 0)),
                  pl.BlockSpec((1, 1, seq, MLA_QK), lambda b, h, i: (b, h, 0, 0)),
                  pl.BlockSpec((1, 1, seq, MLA_V), lambda b, h, i: (b, h, 0, 0))],
        out_specs=pl.BlockSpec((1, tq, MLA_V), lambda b, h, i: (b, i, h)),
        out_shape=jax.ShapeDtypeStruct((bsz, seq, nh * MLA_V), BF16),
        compiler_params=_cp("parallel", "parallel", "parallel"),
        name="mla_attention",
    )(q, k, v)


def _s5_params(a_re, a_im, log_step, b_re, b_im, c_re, c_im, d):
    q = S5_CHUNK
    g, p, hh = S5_GROUPS, S5_STATE, S5_GROUP
    ks = jnp.arange(q + 1, dtype=F32)[:, None, None]
    out = {}
    lag_k, wst, cps = [], [], []
    for direction in range(2):
        ar, ai = a_re[direction].astype(F32), a_im[direction].astype(F32)
        step = jnp.exp(log_step[direction].astype(F32))[:, None]
        mag = jnp.exp(step * ar)
        abar_r = mag * jnp.cos(step * ai)
        abar_i = mag * jnp.sin(step * ai)
        den = ar * ar + ai * ai
        nr = abar_r - 1.0
        fr = (nr * ar + abar_i * ai) / den
        fi = (abar_i * ar - nr * ai) / den
        pmag = jnp.exp(ks * (step * ar)[None])
        pw_r = pmag * jnp.cos(ks * (step * ai)[None])
        pw_i = pmag * jnp.sin(ks * (step * ai)[None])
        br, bi = b_re[direction].astype(F32), b_im[direction].astype(F32)
        bf_r = fr[..., None] * br - fi[..., None] * bi
        bf_i = fr[..., None] * bi + fi[..., None] * br
        cr, ci = c_re[direction].astype(F32), c_im[direction].astype(F32)
        cp_r = cr[None] * pw_r[:, :, None, :] - ci[None] * pw_i[:, :, None, :]
        cp_i = cr[None] * pw_i[:, :, None, :] + ci[None] * pw_r[:, :, None, :]
        lag = (jnp.einsum('kghp,gpj->kghj', cp_r[:q], bf_r, precision='highest')
               - jnp.einsum('kghp,gpj->kghj', cp_i[:q], bf_i, precision='highest'))
        lag_k.append(lag)
        tpow = (q - 1 - jnp.arange(q)) if direction == 0 else jnp.arange(q)
        wr = pw_r[tpow][:, :, :, None] * bf_r[None] - pw_i[tpow][:, :, :, None] * bf_i[None]
        wi = pw_r[tpow][:, :, :, None] * bf_i[None] + pw_i[tpow][:, :, :, None] * bf_r[None]
        wst.extend([wr, wi])
        opow = (jnp.arange(q) + 1) if direction == 0 else (q - jnp.arange(q))
        cps.extend([cp_r[opow], -cp_i[opow]])
        out[f'a_r{direction}'] = pw_r[q].reshape(1, g * p)
        out[f'a_i{direction}'] = pw_i[q].reshape(1, g * p)
    j = S5_OCT
    no = g // j
    dmat = d.astype(F32).reshape(g, hh)[:, :, None] * jnp.eye(hh, dtype=F32)[None]
    kall = jnp.concatenate([lag_k[1][:0:-1], (lag_k[0][0] + lag_k[1][0] + dmat)[None], lag_k[0][1:]], axis=0)
    ks = kall.reshape(2 * q - 1, no, j, hh, hh).transpose(1, 0, 4, 2, 3).reshape(no, 2 * q - 1, 1, hh, j * hh)
    same = (jnp.arange(j)[:, None] == jnp.arange(j * hh)[None, :] // hh)[None, None, :, None, :]
    out['blk'] = jnp.where(same, ks, 0.0).reshape(no, 2 * q - 1, j * hh, j * hh).astype(BF16)
    out['wstc'] = (jnp.stack(wst).reshape(4, q, no, j, p, hh).transpose(2, 1, 3, 5, 0, 4)
                   .reshape(no, q * j * hh, 4 * p).astype(BF16))
    out['cpc'] = (jnp.stack(cps).reshape(4, q, no, j, hh, p).transpose(2, 0, 3, 5, 1, 4)
                  .reshape(no, 4, j * p, q * hh).astype(BF16))
    return out


def _expand_consts():
    q, hh, p, j = S5_CHUNK, S5_GROUP, S5_STATE, S5_OCT
    e_state = np.zeros((4, p, 4, j, p), np.float32)
    for part in range(4):
        e_state[part, np.arange(p), part, :, np.arange(p)] = 1.0
    e_out = np.zeros((q, hh, q, j, hh), np.float32)
    for t in range(q):
        e_out[t, np.arange(hh), t, :, np.arange(hh)] = 1.0
    return (jnp.asarray(e_state.reshape(4 * p, 4 * j * p), BF16),
            jnp.asarray(e_out.reshape(q * hh, q * j * hh), BF16))


def _s5_state_kernel(x_ref, wc_ref, e_ref, vfr_ref, vfi_ref, vbr_ref, vbi_ref, w_scr):
    q, hh, p, j = S5_CHUNK, S5_GROUP, S5_STATE, S5_OCT

    @pl.when(pl.program_id(1) == 0)
    def _():
        shape = (j * hh, 4 * j * p)
        row_j = lax.broadcasted_iota(jnp.int32, shape, 0) // hh
        col_j = (lax.broadcasted_iota(jnp.int32, shape, 1) // p) % j
        same = row_j == col_j
        for t in range(q):
            rows = slice(t * j * hh, (t + 1) * j * hh)
            w = jnp.dot(wc_ref[0, rows, :], e_ref[...], preferred_element_type=F32)
            w_scr[rows, :] = jnp.where(same, w, 0.0).astype(BF16)

    v = jnp.dot(x_ref[...], w_scr[...], preferred_element_type=F32)
    w = vfr_ref.shape[1]
    vfr_ref[...] = v[:, 0 * w:1 * w]
    vfi_ref[...] = v[:, 1 * w:2 * w]
    vbr_ref[...] = v[:, 2 * w:3 * w]
    vbi_ref[...] = v[:, 3 * w:4 * w]


def _s5_scan_kernel(vfr_ref, vfi_ref, vbr_ref, vbi_ref, afr_ref, afi_ref, abr_ref, abi_ref,
                    sfr_ref, sfi_ref, sbr_ref, sbi_ref):
    nc = vfr_ref.shape[1]
    afr, afi, abr, abi = afr_ref[...], afi_ref[...], abr_ref[...], abi_ref[...]

    def body(c, carry):
        fr, fi, br, bi = carry
        cb = nc - 1 - c
        sfr_ref[0, pl.ds(c, 1), :] = fr
        sfi_ref[0, pl.ds(c, 1), :] = fi
        sbr_ref[0, pl.ds(cb, 1), :] = br
        sbi_ref[0, pl.ds(cb, 1), :] = bi
        nfr = afr * fr - afi * fi + vfr_ref[0, pl.ds(c, 1), :]
        nfi = afr * fi + afi * fr + vfi_ref[0, pl.ds(c, 1), :]
        nbr = abr * br - abi * bi + vbr_ref[0, pl.ds(cb, 1), :]
        nbi = abr * bi + abi * br + vbi_ref[0, pl.ds(cb, 1), :]
        return nfr, nfi, nbr, nbi

    z = jnp.zeros((1, vfr_ref.shape[2]), F32)
    lax.fori_loop(0, nc, body, (z, z, z, z))


def _s5_out_kernel(x_ref, blk_ref, sfr_ref, sfi_ref, sbr_ref, sbi_ref, cpc_ref, e_ref, y_ref,
                   m_scr, cp_scr):
    q, hh, p, j = S5_CHUNK, S5_GROUP, S5_STATE, S5_OCT
    bw = j * hh

    @pl.when(pl.program_id(1) == 0)
    def _():
        for ti in range(q):
            for to in range(q):
                m_scr[ti * bw:(ti + 1) * bw, to * bw:(to + 1) * bw] = blk_ref[0, to - ti + q - 1]
        shape = (j * p, q * bw)
        row_j = lax.broadcasted_iota(jnp.int32, shape, 0) // p
        col_j = (lax.broadcasted_iota(jnp.int32, shape, 1) // hh) % j
        same = row_j == col_j
        for part in range(4):
            c = jnp.dot(cpc_ref[0, part], e_ref[...], preferred_element_type=F32)
            cp_scr[part] = jnp.where(same, c, 0.0).astype(BF16)

    y = jnp.dot(x_ref[...], m_scr[...], preferred_element_type=F32)
    for part, s_ref in enumerate((sfr_ref, sfi_ref, sbr_ref, sbi_ref)):
        y += jnp.dot(s_ref[...].astype(BF16), cp_scr[part], preferred_element_type=F32)
    y_ref[...] = y


def _s5_post_kernel(y_ref, w_ref, b_ref, perm_ref, o_ref):
    oct_w = S5_CHUNK * LANE
    y = jnp.concatenate(
        [jnp.concatenate([y_ref[:, o * oct_w + t * LANE:o * oct_w + (t + 1) * LANE]
                          for o in range(S5_W // LANE)], axis=1)
         for t in range(S5_CHUNK)], axis=0)
    y = jax.nn.gelu(y)
    z = jnp.dot(y.astype(BF16), w_ref[...], preferred_element_type=F32) + b_ref[...]
    o = (y * jax.nn.sigmoid(z)).astype(BF16)
    o_ref[...] = jnp.dot(perm_ref[...], o, preferred_element_type=F32).astype(o_ref.dtype)


def _s5(xs, prm, glu_w, glu_b):
    bsz, nc, _ = xs.shape
    q, g, p = S5_CHUNK, S5_GROUPS, S5_STATE
    seq = nc * q
    no = g // S5_OCT
    rows = bsz * nc
    feat = q * LANE
    sw = S5_OCT * p
    nst = g * p
    x = xs.reshape(rows, no * feat)
    e_state, e_out = _expand_consts()
    rt = min(512, rows)
    xspec = pl.BlockSpec((rt, feat), lambda o, i: (i, o))
    vspec = pl.BlockSpec((rt, sw), lambda o, i: (i, o))
    vshape = jax.ShapeDtypeStruct((rows, nst), F32)
    v4 = pl.pallas_call(
        _s5_state_kernel,
        grid=(no, rows // rt),
        in_specs=[xspec, pl.BlockSpec((1, feat, 4 * p), lambda o, i: (o, 0, 0)),
                  pl.BlockSpec(e_state.shape, lambda o, i: (0, 0))],
        out_specs=[vspec] * 4,
        out_shape=[vshape] * 4,
        scratch_shapes=[pltpu.VMEM((feat, 4 * sw), BF16)],
        compiler_params=_cp("parallel", "arbitrary"),
        name="s5_state",
    )(x, prm['wstc'], e_state)

    lt = min(512, nst)
    v4 = [v.reshape(bsz, nc, nst) for v in v4]
    sspec = pl.BlockSpec((1, nc, lt), lambda b, n: (b, 0, n))
    aspec = pl.BlockSpec((1, lt), lambda b, n: (0, n))
    s4 = pl.pallas_call(
        _s5_scan_kernel,
        grid=(bsz, nst // lt),
        in_specs=[sspec] * 4 + [aspec] * 4,
        out_specs=[sspec] * 4,
        out_shape=[jax.ShapeDtypeStruct((bsz, nc, nst), F32)] * 4,
        compiler_params=_cp("parallel", "parallel"),
        name="s5_scan",
    )(*v4, prm['a_r0'], prm['a_i0'], prm['a_r1'], prm['a_i1'])
    s4 = [s.reshape(rows, nst) for s in s4]

    ro = min(256, rows)
    xspec = pl.BlockSpec((ro, feat), lambda o, i: (i, o))
    vspec = pl.BlockSpec((ro, sw), lambda o, i: (i, o))
    blk, cpc = prm['blk'], prm['cpc']
    y = pl.pallas_call(
        _s5_out_kernel,
        grid=(no, rows // ro),
        in_specs=([xspec, pl.BlockSpec((1,) + blk.shape[1:], lambda o, i: (o, 0, 0, 0))] + [vspec] * 4
                  + [pl.BlockSpec((1,) + cpc.shape[1:], lambda o, i: (o, 0, 0, 0)),
                     pl.BlockSpec(e_out.shape, lambda o, i: (0, 0))]),
        out_specs=xspec,
        out_shape=jax.ShapeDtypeStruct((rows, no * feat), F32),
        scratch_shapes=[pltpu.VMEM((feat, feat), BF16), pltpu.VMEM((4, sw, feat), BF16)],
        compiler_params=_cp("parallel", "arbitrary"),
        name="s5_out",
    )(x, blk, *s4, cpc, e_out)

    cn = min(32, rows)
    const = lambda shp: pl.BlockSpec(shp, lambda i: (0,) * len(shp))
    out = pl.pallas_call(
        _s5_post_kernel,
        grid=(rows // cn,),
        in_specs=[pl.BlockSpec((cn, no * feat), lambda i: (i, 0)),
                  const((S5_W, S5_W)), const((1, S5_W)), const((cn * q, cn * q))],
        out_specs=pl.BlockSpec((cn * q, S5_W), lambda i: (i, 0)),
        out_shape=jax.ShapeDtypeStruct((rows * q, S5_W), BF16),
        compiler_params=_cp("parallel"),
        name="s5_post",
    )(y, glu_w.astype(BF16), glu_b.reshape(1, -1).astype(F32), _chunk_perm(cn, False))
    return out.reshape(bsz, seq, S5_W)


def _hy_prep_kernel(x0_ref, x1_ref, v_ref, w_ref, b_ref, x0c_ref, gv_ref):
    seq = x0_ref.shape[1]
    row = lax.broadcasted_iota(jnp.int32, (seq, LANE), 0)
    first = row == 0
    last = row == seq - 1

    def conv(ref, k):
        u = ref[0]
        w = w_ref[k]
        prev = jnp.where(first, 0.0, pltpu.roll(u, 1, 0))
        nxt = jnp.where(last, 0.0, pltpu.roll(u, seq - 1, 0))
        return prev * w[0:1] + u * w[1:2] + nxt * w[2:3] + b_ref[k]

    x0c_ref[0] = conv(x0_ref, 0)
    gv_ref[0] = conv(x1_ref, 1) * conv(v_ref, 2)


def _hy_prep(proj, conv_w, conv_b):
    bsz, seq, _ = proj.shape
    nt = HY_W // LANE
    base = COL_HY // LANE
    w = conv_w.astype(F32).reshape(3, 3, HY_W).transpose(1, 0, 2)
    b = conv_b.astype(F32).reshape(3, 1, HY_W)
    col = lambda part: pl.BlockSpec((1, seq, LANE), lambda bb, c: (bb, 0, base + part * nt + c))
    ospec = pl.BlockSpec((1, seq, LANE), lambda bb, c: (bb, 0, c))
    return pl.pallas_call(
        _hy_prep_kernel,
        grid=(bsz, nt),
        in_specs=[col(0), col(1), col(2),
                  pl.BlockSpec((3, 3, LANE), lambda bb, c: (0, 0, c)),
                  pl.BlockSpec((3, 1, LANE), lambda bb, c: (0, 0, c))],
        out_specs=[ospec, ospec],
        out_shape=[jax.ShapeDtypeStruct((bsz, seq, HY_W), F32)] * 2,
        compiler_params=_cp("parallel", "parallel"),
        name="hyena_prep",
    )(proj, proj, proj, w, b)


def _hy_filter_kernel(feat_ref, w1_ref, b1_ref, fq_ref, w2_ref, b2_ref, w3f_ref, w3b_ref,
                      dl_ref, o_ref, h_scr):
    @pl.when(pl.program_id(0) == 0)
    def _():
        fq = fq_ref[...]
        h1 = jnp.sin(fq[0:1] * (jnp.dot(feat_ref[...], w1_ref[...], preferred_element_type=F32) + b1_ref[...]))
        h_scr[...] = jnp.sin(fq[1:2] * (jnp.dot(h1, w2_ref[...], preferred_element_type=F32) + b2_ref[...]))

    h = h_scr[...]
    win = jnp.exp(-feat_ref[:, 0:1] * dl_ref[...])
    hf = jnp.dot(h, w3f_ref[...], preferred_element_type=F32) * win
    hb = jnp.dot(h, w3b_ref[...], preferred_element_type=F32) * win
    row = lax.broadcasted_iota(jnp.int32, hb.shape, 0)
    hb = jnp.where(row == 0, 0.0, hb)
    ss = jnp.sum(hf * hf, axis=0, keepdims=True) + jnp.sum(hb * hb, axis=0, keepdims=True)
    scale = lax.rsqrt(ss + FILTER_EPS)
    o_ref[0] = hf * scale
    o_ref[1] = hb * scale


def _hy_filter(seq, w1, b1, freq, w2, b2, w3):
    t01 = jnp.linspace(0.0, 1.0, seq, dtype=F32)[:, None]
    w = 2.0 * math.pi * jnp.arange(seq, dtype=F32)[:, None] / seq
    bands = jnp.linspace(1e-4, HY_BANDS - 1, HY_BANDS, dtype=F32)[None, :]
    fw = w * bands
    emb = 1 + 2 * HY_BANDS
    kpad = 64
    feat = jnp.concatenate([t01, jnp.cos(fw), -jnp.sin(fw), jnp.zeros((seq, kpad - emb), F32)], axis=-1)
    w1p = jnp.zeros((kpad, w1.shape[1]), F32).at[:emb].set(w1.astype(F32))
    deltas = jnp.abs(jnp.linspace(math.log(HY_DECAY_TARGET) / HY_SLOW_PCT,
                                  math.log(HY_DECAY_TARGET) / HY_FAST_PCT, HY_W, dtype=F32))[None, :]
    nt = HY_W // LANE
    hid = w2.shape[0]
    const = lambda shp: pl.BlockSpec(shp, lambda c: (0,) * len(shp))
    return pl.pallas_call(
        _hy_filter_kernel,
        grid=(nt,),
        in_specs=[const((seq, kpad)), const((kpad, hid)), const((1, hid)), const((2, hid)),
                  const((hid, hid)), const((1, hid)),
                  pl.BlockSpec((hid, LANE), lambda c: (0, c)),
                  pl.BlockSpec((hid, LANE), lambda c: (0, nt + c)),
                  pl.BlockSpec((1, LANE), lambda c: (0, c))],
        out_specs=pl.BlockSpec((2, seq, LANE), lambda c: (0, 0, c)),
        out_shape=jax.ShapeDtypeStruct((2, seq, HY_W), F32),
        scratch_shapes=[pltpu.VMEM((seq, hid), F32)],
        compiler_params=_cp("arbitrary"),
        name="hyena_filter",
    )(feat, w1p, b1.reshape(1, -1).astype(F32), freq.astype(F32), w2.astype(F32),
      b2.reshape(1, -1).astype(F32), w3.astype(F32), w3.astype(F32), deltas)


def _dft_tables(n):
    big = n * n
    k1 = jnp.arange(n, dtype=jnp.int32)
    ph1 = (2.0 * math.pi / n) * ((k1[:, None] * k1[None, :n // 2]) % n).astype(F32)
    f1 = jnp.stack([jnp.cos(ph1), -jnp.sin(ph1)], axis=1).reshape(2 * n, n // 2)
    f4 = f1.T / big
    eye = jnp.eye(SUBLANE, dtype=F32)
    f1 = jnp.einsum('kn,rs->krns', f1, eye).reshape(2 * n * SUBLANE, (n // 2) * SUBLANE)
    f4 = jnp.einsum('nk,rs->nrks', f4, eye).reshape((n // 2) * SUBLANE, 2 * n * SUBLANE)
    idx = (k1[None, None, :] * (k1[:, None, None] + n * k1[None, :, None])) % big
    th = (2.0 * math.pi / big) * idx.astype(F32)
    gr, gi = jnp.cos(th), -jnp.sin(th)
    gfwd = jnp.concatenate([jnp.concatenate([gr, -gi], axis=2),
                            jnp.concatenate([gi, gr], axis=2)], axis=1)
    hr, hi = jnp.cos(th).transpose(0, 2, 1), jnp.sin(th).transpose(0, 2, 1)
    ginv = jnp.concatenate([jnp.concatenate([hr, -hi], axis=2),
                            jnp.concatenate([hi, hr], axis=2)], axis=1)
    return f1.astype(BF16), f4.astype(BF16), gfwd.astype(BF16), ginv.astype(BF16)


DFT_ROWS = 2 * SUBLANE


def _kron_dot(f_ref, x3):
    k, _, ch = x3.shape
    halves = []
    for h in range(DFT_ROWS // SUBLANE):
        xh = x3[:, h * SUBLANE:(h + 1) * SUBLANE, :].reshape(k * SUBLANE, ch).astype(BF16)
        yh = jnp.dot(f_ref[...], xh, preferred_element_type=F32)
        halves.append(yh.reshape(-1, SUBLANE, ch))
    return jnp.concatenate(halves, axis=1)


def _dft1_kernel(x_ref, f_ref, o_ref):
    o_ref[0, :, 0] = _kron_dot(f_ref, x_ref[0, :, 0]).astype(o_ref.dtype)


def _dft1(x, f1, n):
    bsz, seq, ch = x.shape
    nb = n // DFT_ROWS
    return pl.pallas_call(
        _dft1_kernel,
        grid=(bsz, nb),
        in_specs=[pl.BlockSpec((1, n // 2, 1, DFT_ROWS, ch), lambda b, j: (b, 0, j, 0, 0)),
                  pl.BlockSpec(f1.shape, lambda b, j: (0, 0))],
        out_specs=pl.BlockSpec((1, 2 * n, 1, DFT_ROWS, ch), lambda b, j: (b, 0, j, 0, 0)),
        out_shape=jax.ShapeDtypeStruct((bsz, 2 * n, nb, DFT_ROWS, ch), BF16),
        compiler_params=_cp("parallel", "parallel"),
        name="dft_stage1",
    )(x.reshape(bsz, n // 2, nb, DFT_ROWS, ch), f1)


def _filt_spec_kernel(af_ref, ab_ref, g_ref, hre_ref, him_ref):
    n = g_ref.shape[1] // 2
    for i in range(g_ref.shape[0]):
        xf = jnp.dot(g_ref[i], af_ref[0, i], preferred_element_type=F32)
        xb = jnp.dot(g_ref[i], ab_ref[0, i], preferred_element_type=F32)
        hre_ref[i] = xf[:n] + xb[:n]
        him_ref[i] = xf[n:] - xb[n:]


def _conv_spec_kernel(a_ref, gf_ref, gi_ref, hre_ref, him_ref, c_ref):
    n = gf_ref.shape[1] // 2
    for i in range(gf_ref.shape[0]):
        x = jnp.dot(gf_ref[i], a_ref[0, i], preferred_element_type=F32)
        xr, xi = x[:n], x[n:]
        hr, hi = hre_ref[i], him_ref[i]
        pr = xr * hr - xi * hi
        pi = xr * hi + xi * hr
        pc = jnp.concatenate([pr, pi], axis=0).astype(BF16)
        c_ref[0, i] = jnp.dot(gi_ref[i], pc, preferred_element_type=F32).astype(c_ref.dtype)


def _dft4_kernel(c_ref, f_ref, x0_ref, gv_ref, bias_ref, o_ref):
    y = _kron_dot(f_ref, c_ref[0, :, 0].astype(F32))
    o_ref[0, :, 0] = (x0_ref[0, :, 0] * (y + gv_ref[0, :, 0] * bias_ref[...])).astype(o_ref.dtype)


def _hyena(proj, tables, conv_w, conv_b, f_w1, f_b1, f_freq, f_w2, f_b2, f_w3, f_bias):
    bsz, seq, _ = proj.shape
    n = math.isqrt(2 * seq)
    assert n * n == 2 * seq and n % 2 == 0
    f1, f4, gfwd, ginv = tables
    ch = HY_W
    kb = min(8, n)
    x0c, gv = _hy_prep(proj, conv_w, conv_b)
    filt = _hy_filter(seq, f_w1, f_b1, f_freq, f_w2, f_b2, f_w3)

    a_f = _dft1(filt, f1, n).reshape(2, n, 2 * n, ch)
    adir = lambda bb: pl.BlockSpec((1, kb, 2 * n, ch), lambda k: (bb, k, 0, 0))
    gspec1 = pl.BlockSpec((kb, 2 * n, 2 * n), lambda k: (k, 0, 0))
    hspec1 = pl.BlockSpec((kb, n, ch), lambda k: (k, 0, 0))
    h_re, h_im = pl.pallas_call(
        _filt_spec_kernel,
        grid=(n // kb,),
        in_specs=[adir(0), adir(1), gspec1],
        out_specs=[hspec1, hspec1],
        out_shape=[jax.ShapeDtypeStruct((n, n, ch), F32)] * 2,
        compiler_params=_cp("parallel"),
        name="hyena_filter_spectrum",
    )(a_f, a_f, gfwd)

    a = _dft1(gv, f1, n).reshape(bsz, n, 2 * n, ch)
    aspec = pl.BlockSpec((1, kb, 2 * n, ch), lambda k, b: (b, k, 0, 0))
    gspec = pl.BlockSpec((kb, 2 * n, 2 * n), lambda k, b: (k, 0, 0))
    hspec = pl.BlockSpec((kb, n, ch), lambda k, b: (k, 0, 0))
    c = pl.pallas_call(
        _conv_spec_kernel,
        grid=(n // kb, bsz),
        in_specs=[aspec, gspec, gspec, hspec, hspec],
        out_specs=aspec,
        out_shape=jax.ShapeDtypeStruct((bsz, n, 2 * n, ch), BF16),
        compiler_params=_cp("parallel", "parallel"),
        name="hyena_conv_spectrum",
    )(a, gfwd, ginv, h_re, h_im).reshape(bsz, 2 * n, n // DFT_ROWS, DFT_ROWS, ch)

    nb = n // DFT_ROWS
    nat = lambda arr: arr.reshape(bsz, n // 2, nb, DFT_ROWS, ch)
    rspec = pl.BlockSpec((1, n // 2, 1, DFT_ROWS, ch), lambda b, j: (b, 0, j, 0, 0))
    out = pl.pallas_call(
        _dft4_kernel,
        grid=(bsz, nb),
        in_specs=[pl.BlockSpec((1, 2 * n, 1, DFT_ROWS, ch), lambda b, j: (b, 0, j, 0, 0)),
                  pl.BlockSpec(f4.shape, lambda b, j: (0, 0)),
                  rspec, rspec,
                  pl.BlockSpec((1, ch), lambda b, j: (0, 0))],
        out_specs=rspec,
        out_shape=jax.ShapeDtypeStruct((bsz, n // 2, nb, DFT_ROWS, ch), BF16),
        compiler_params=_cp("parallel", "parallel"),
        name="dft_stage4",
    )(c, f4, nat(x0c), nat(gv), f_bias.astype(F32).reshape(1, ch))
    return out.reshape(bsz, seq, ch)


def _outproj_kernel(om_ref, os_ref, oh_ref, x_ref, mod_ref, w_ref, g_ref, b_ref, o_ref):
    n_m, n_s = om_ref.shape[2], os_ref.shape[2]
    mixed = jnp.dot(om_ref[0], w_ref[0, 0:n_m], preferred_element_type=F32)
    mixed += jnp.dot(os_ref[0], w_ref[0, n_m:n_m + n_s], preferred_element_type=F32)
    mixed += jnp.dot(oh_ref[0], w_ref[0, n_m + n_s:], preferred_element_type=F32)
    gate = mod_ref[0][2:3]
    o_ref[0] = _layer_norm(ALPHA * x_ref[0] + gate * mixed, g_ref[...], b_ref[...])


def _outproj(o_mla, o_s5, o_hy, x, mod, w, l, ln_g, ln_b):
    bsz, seq, d = x.shape
    tm = min(512, seq)
    row = lambda width: pl.BlockSpec((1, tm, width), lambda b, i: (b, i, 0))
    const = lambda shp: pl.BlockSpec(shp, lambda b, i: (0,) * len(shp))
    return pl.pallas_call(
        _outproj_kernel,
        grid=(bsz, seq // tm),
        in_specs=[row(o_mla.shape[2]), row(o_s5.shape[2]), row(o_hy.shape[2]), row(d),
                  pl.BlockSpec((1, 6, d), lambda b, i: (b, 0, 0)),
                  pl.BlockSpec((1,) + w.shape[1:], lambda b, i: (l, 0, 0)), const((1, d)), const((1, d))],
        out_specs=row(d),
        out_shape=jax.ShapeDtypeStruct((bsz, seq, d), F32),
        compiler_params=_cp("parallel", "parallel"),
        name="out_proj_ln",
    )(o_mla, o_s5, o_hy, x, mod, w, ln_g.reshape(1, -1), ln_b.reshape(1, -1))


def _ffn_kernel(xp_ref, x_ref, xn_ref, mod_ref, wg_ref, wu_ref, cw_ref, cb_ref, wd_ref,
                g_ref, b_ref, o_ref, u_scr, acc_scr):
    i, j = pl.program_id(1), pl.program_id(2)
    tm = x_ref.shape[1]
    halo = xp_ref.shape[1]

    @pl.when(j == 0)
    def _():
        m = mod_ref[0]
        sc, sh = 1.0 + m[4:5], m[3:4]
        u_scr[0:halo] = (xp_ref[0] * sc + sh).astype(BF16)
        u_scr[halo:halo + tm] = (x_ref[0] * sc + sh).astype(BF16)
        u_scr[halo + tm:] = (xn_ref[0] * sc + sh).astype(BF16)
        acc_scr[...] = jnp.zeros_like(acc_scr)

    u = u_scr[...]
    rows = tm + 2 * halo
    gx = jnp.dot(u, wg_ref[0], preferred_element_type=F32)
    r = lax.broadcasted_iota(jnp.int32, (tm, 1), 0)
    keep_prev = jnp.logical_or(r > 0, i > 0)
    keep_next = jnp.logical_or(r < tm - 1, i < pl.num_programs(1) - 1)
    g_prev = jnp.where(keep_prev, pltpu.roll(gx, 1, 0)[halo:halo + tm], 0.0)
    g_next = jnp.where(keep_next, pltpu.roll(gx, rows - 1, 0)[halo:halo + tm], 0.0)
    cw = cw_ref[...]
    conv = g_prev * cw[0:1] + gx[halo:halo + tm] * cw[1:2] + g_next * cw[2:3] + cb_ref[...]
    up = jnp.dot(u[halo:halo + tm], wu_ref[0], preferred_element_type=F32)
    h = (conv * jax.nn.sigmoid(conv) * up).astype(BF16)
    acc_scr[...] += jnp.dot(h, wd_ref[0], preferred_element_type=F32)

    @pl.when(j == pl.num_programs(2) - 1)
    def _():
        gate = mod_ref[0][5:6]
        o_ref[0] = _layer_norm(ALPHA * x_ref[0] + gate * acc_scr[...], g_ref[...], b_ref[...])


def _ffn(x, mod, wg, wu, conv_w, conv_b, wd, l, ln_g, ln_b):
    bsz, seq, d = x.shape
    ff = wg.shape[2]
    tm = min(512, seq)
    tn = 512
    halo = SUBLANE
    nhb = seq // halo
    per = tm // halo
    const = lambda shp: pl.BlockSpec(shp, lambda b, i, j: (0,) * len(shp))
    return pl.pallas_call(
        _ffn_kernel,
        grid=(bsz, seq // tm, ff // tn),
        in_specs=[pl.BlockSpec((1, halo, d), lambda b, i, j: (b, jnp.maximum(i * per - 1, 0), 0)),
                  pl.BlockSpec((1, tm, d), lambda b, i, j: (b, i, 0)),
                  pl.BlockSpec((1, halo, d), lambda b, i, j: (b, jnp.minimum((i + 1) * per, nhb - 1), 0)),
                  pl.BlockSpec((1, 6, d), lambda b, i, j: (b, 0, 0)),
                  pl.BlockSpec((1, d, tn), lambda b, i, j: (l, 0, j)),
                  pl.BlockSpec((1, d, tn), lambda b, i, j: (l, 0, j)),
                  pl.BlockSpec((3, tn), lambda b, i, j: (0, j)),
                  pl.BlockSpec((1, tn), lambda b, i, j: (0, j)),
                  pl.BlockSpec((1, tn, d), lambda b, i, j: (l, j, 0)),
                  const((1, d)), const((1, d))],
        out_specs=pl.BlockSpec((1, tm, d), lambda b, i, j: (b, i, 0)),
        out_shape=jax.ShapeDtypeStruct((bsz, seq, d), F32),
        scratch_shapes=[pltpu.VMEM((tm + 2 * halo, d), BF16), pltpu.VMEM((tm, d), F32)],
        compiler_params=_cp("parallel", "parallel", "arbitrary"),
        name="conv_ffn_ln",
    )(x, x, x, mod, wg, wu, conv_w.astype(F32), conv_b.reshape(1, -1).astype(F32), wd,
      ln_g.reshape(1, -1), ln_b.reshape(1, -1))


def _rot_half_cols(w):
    half = w.shape[-1] // 2
    return jnp.concatenate([-w[..., half:], w[..., :half]], axis=-1)


def _prep_w_in(w):
    q, kv, kr, s5, hy = jnp.split(w, (512, 768, 832, 1344), axis=-1)
    return jnp.concatenate([q, hy, kv, kr, _rot_half_cols(kr), s5], axis=-1).astype(BF16)


def _prep_w_uq(w):
    w = w.reshape(w.shape[:-1] + (MLA_HEADS, MLA_QK))
    pe = w[..., MLA_NOPE:]
    return jnp.concatenate([w, _rot_half_cols(pe)], axis=-1).reshape(w.shape[:-2] + (-1,)).astype(BF16)


def kernel(x, c, positions, ada_w, ada_b, w_in, q_norm_g, kv_norm_g, w_uq, w_ukv, s5_a_re, s5_a_im, s5_log_step, s5_b_re, s5_b_im, s5_c_re, s5_c_im, s5_d, s5_glu_w, s5_glu_b, hy_conv_w, hy_conv_b, hy_f_w1, hy_f_b1, hy_f_freq, hy_f_w2, hy_f_b2, hy_f_w3, hy_f_bias, w_out, ln1_g, ln1_b, ffn_w_gate, ffn_w_up, ffn_conv_w, ffn_conv_b, ffn_w_down, ln2_g, ln2_b):
    bsz, seq, _ = x.shape
    depth = ada_w.shape[0]
    mods = _ada(c, ada_w, ada_b)
    tables = _dft_tables(math.isqrt(2 * seq))
    w_in_b = _prep_w_in(w_in)
    w_uq_b = _prep_w_uq(w_uq)
    w_ukv_b = w_ukv.astype(BF16)
    w_out_b = w_out.astype(BF16)
    wg_b, wu_b, wd_b = ffn_w_gate.astype(BF16), ffn_w_up.astype(BF16), ffn_w_down.astype(BF16)
    s5_all = jax.vmap(_s5_params)(s5_a_re, s5_a_im, s5_log_step, s5_b_re, s5_b_im, s5_c_re, s5_c_im, s5_d)
    for l in range(depth):
        mod = mods[l]
        proj, xs5 = _inproj(x, mod, w_in_b, l)
        q, k, v = _mla_prep(proj, positions, q_norm_g[l], kv_norm_g[l], w_uq_b[l], w_ukv_b[l])
        o_mla = _attention(q, k, v)
        o_s5 = _s5(xs5, {name: t[l] for name, t in s5_all.items()}, s5_glu_w[l], s5_glu_b[l])
        o_hy = _hyena(proj, tables, hy_conv_w[l], hy_conv_b[l], hy_f_w1[l], hy_f_b1[l],
                      hy_f_freq[l], hy_f_w2[l], hy_f_b2[l], hy_f_w3[l], hy_f_bias[l])
        x = _outproj(o_mla, o_s5, o_hy, x, mod, w_out_b, l, ln1_g[l], ln1_b[l])
        x = _ffn(x, mod, wg_b, wu_b, ffn_conv_w[l], ffn_conv_b[l], wd_b, l, ln2_g[l], ln2_b[l])
    return x
```

```python
import functools
import math

import numpy as np
import jax
import jax.numpy as jnp
from jax import lax
from jax.experimental import pallas as pl
from jax.experimental.pallas import tpu as pltpu

F32 = jnp.float32
BF16 = jnp.bfloat16

MLA_HEADS = 8
MLA_NOPE = 128
MLA_ROPE = 64
MLA_V = 128
MLA_QK = MLA_NOPE + MLA_ROPE
MLA_Q_RANK = 512
MLA_KV_RANK = 256
ROPE_THETA = 10000.0
S5_W = 512
S5_GROUP = 16
S5_GROUPS = 32
S5_STATE = 64
S5_CHUNK = 16
S5_OCT = 8
HY_W = 512
HY_BANDS = 16
HY_DECAY_TARGET = 1e-2
HY_FAST_PCT = 0.3
HY_SLOW_PCT = 1.5
LN_EPS = 1e-5
RMS_EPS = 1e-6
FILTER_EPS = 1e-6
DEPTH = 2
ALPHA = (2 * DEPTH) ** 0.25

COL_Q = 0
COL_HY = 512
COL_KV = 2048
COL_KR = 2304
COL_S5 = 2432
IN_COLS_PAD = 2944

LANE = 128
SUBLANE = 8
VMEM_LIMIT = 56 * 1024 * 1024


def _cp(*sem):
    return pltpu.CompilerParams(dimension_semantics=sem, vmem_limit_bytes=VMEM_LIMIT)


def _layer_norm(y, g, b):
    mu = jnp.mean(y, axis=-1, keepdims=True)
    d = y - mu
    var = jnp.mean(d * d, axis=-1, keepdims=True)
    return d * lax.rsqrt(var + LN_EPS) * g + b


def _ada_kernel(c_ref, w_ref, b_ref, o_ref):
    c = c_ref[...]
    cond = c * jax.nn.sigmoid(c)
    o_ref[0] = jnp.dot(cond, w_ref[0], preferred_element_type=F32) + b_ref[0]


def _ada(c, ada_w, ada_b):
    bsz, d = c.shape
    depth, _, n = ada_w.shape
    tn = 1024
    cp = jnp.zeros((SUBLANE, d), F32).at[:bsz].set(c)
    out = pl.pallas_call(
        _ada_kernel,
        grid=(depth, n // tn),
        in_specs=[pl.BlockSpec((SUBLANE, d), lambda l, j: (0, 0)),
                  pl.BlockSpec((1, d, tn), lambda l, j: (l, 0, j)),
                  pl.BlockSpec((1, 1, tn), lambda l, j: (l, 0, j))],
        out_specs=pl.BlockSpec((1, SUBLANE, tn), lambda l, j: (l, 0, j)),
        out_shape=jax.ShapeDtypeStruct((depth, SUBLANE, n), F32),
        compiler_params=_cp("parallel", "parallel"),
        name="ada_mod",
    )(cp, ada_w, ada_b.reshape(depth, 1, n))
    return out[:, :bsz].reshape(depth, bsz, 6, d)


def _chunk_perm(chunks, to_time_major):
    q = S5_CHUNK
    nat = np.arange(chunks * q).reshape(chunks, q)
    tmaj = nat.T.reshape(-1)
    p = np.zeros((chunks * q, chunks * q), np.float32)
    if to_time_major:
        p[np.arange(chunks * q), tmaj] = 1.0
    else:
        p[tmaj, np.arange(chunks * q)] = 1.0
    return jnp.asarray(p, BF16)


def _inproj_kernel(x_ref, mod_ref, w_ref, perm_ref, o_ref, xs_ref):
    m = mod_ref[0]
    u = x_ref[0] * (1.0 + m[1:2]) + m[0:1]
    res = jnp.dot(u.astype(BF16), w_ref[0], preferred_element_type=F32)
    o_ref[0] = res[:, :COL_S5]
    us = jnp.dot(perm_ref[...], res[:, COL_S5:].astype(BF16), preferred_element_type=F32).astype(BF16)
    cn = xs_ref.shape[1]
    oct_w = S5_CHUNK * LANE
    for t in range(S5_CHUNK):
        for o in range(S5_W // LANE):
            xs_ref[0, :, o * oct_w + t * LANE:o * oct_w + (t + 1) * LANE] = (
                us[t * cn:(t + 1) * cn, o * LANE:(o + 1) * LANE])


def _inproj(x, mod, w, l):
    bsz, seq, d = x.shape
    n = w.shape[2]
    tm = min(256, seq)
    cn = tm // S5_CHUNK
    return pl.pallas_call(
        _inproj_kernel,
        grid=(bsz, seq // tm),
        in_specs=[pl.BlockSpec((1, tm, d), lambda b, i: (b, i, 0)),
                  pl.BlockSpec((1, 6, d), lambda b, i: (b, 0, 0)),
                  pl.BlockSpec((1, d, n), lambda b, i: (l, 0, 0)),
                  pl.BlockSpec((tm, tm), lambda b, i: (0, 0))],
        out_specs=[pl.BlockSpec((1, tm, COL_S5), lambda b, i: (b, i, 0)),
                   pl.BlockSpec((1, cn, S5_CHUNK * S5_W), lambda b, i: (b, i, 0))],
        out_shape=[jax.ShapeDtypeStruct((bsz, seq, COL_S5), F32),
                   jax.ShapeDtypeStruct((bsz, seq // S5_CHUNK, S5_CHUNK * S5_W), BF16)],
        compiler_params=_cp("parallel", "parallel"),
        name="in_proj",
    )(x, mod, w, _chunk_perm(cn, True))


def _mla_prep_kernel(ql_ref, kvl_ref, kr_ref, pos_ref, invf_ref, qg_ref, kvg_ref,
                     wq_ref, wkv_ref, q_ref, k_ref, v_ref):
    scale = MLA_QK ** -0.5 * math.log2(math.e)
    ql = ql_ref[0]
    qn = ql * lax.rsqrt(jnp.mean(ql * ql, axis=-1, keepdims=True) + RMS_EPS) * qg_ref[...]
    kvl = kvl_ref[0]
    kvn = kvl * lax.rsqrt(jnp.mean(kvl * kvl, axis=-1, keepdims=True) + RMS_EPS) * kvg_ref[...]
    ang = pos_ref[0] * invf_ref[...]
    cos = jnp.cos(ang)
    sin = jnp.sin(ang)
    qa = jnp.dot(qn.astype(BF16), wq_ref[...], preferred_element_type=F32)
    kva = jnp.dot(kvn.astype(BF16), wkv_ref[...], preferred_element_type=F32)
    kr = kr_ref[0]
    kpe = (kr * cos + pltpu.roll(kr, MLA_ROPE, 1) * sin)[:, :MLA_ROPE].astype(BF16)
    for h in range(MLA_HEADS):
        c0 = 2 * LANE * h
        pr = qa[:, c0 + LANE:c0 + 2 * LANE]
        qpe = pr * cos + pltpu.roll(pr, MLA_ROPE, 1) * sin
        q_ref[0, h, :, 0:MLA_NOPE] = (qa[:, c0:c0 + LANE] * scale).astype(BF16)
        q_ref[0, h, :, MLA_NOPE:MLA_QK] = (qpe[:, :MLA_ROPE] * scale).astype(BF16)
        k_ref[0, h, :, 0:MLA_NOPE] = kva[:, c0:c0 + LANE].astype(BF16)
        k_ref[0, h, :, MLA_NOPE:MLA_QK] = kpe
        v_ref[0, h] = kva[:, c0 + LANE:c0 + 2 * LANE].astype(BF16)


def _mla_prep(proj, positions, q_g, kv_g, wq, wkv):
    bsz, seq, _ = proj.shape
    tm = min(512, seq)
    inv_freq = ROPE_THETA ** (-jnp.arange(0, MLA_ROPE, 2, dtype=F32) / MLA_ROPE)
    invf = jnp.tile(inv_freq, LANE // (MLA_ROPE // 2)).reshape(1, LANE)
    pos = positions.astype(F32).reshape(bsz, seq, 1)
    hshape = lambda w: jax.ShapeDtypeStruct((bsz, MLA_HEADS, seq, w), BF16)
    hspec = lambda w: pl.BlockSpec((1, MLA_HEADS, tm, w), lambda b, i: (b, 0, i, 0))
    const = lambda shp: pl.BlockSpec(shp, lambda b, i: (0,) * len(shp))
    return pl.pallas_call(
        _mla_prep_kernel,
        grid=(bsz, seq // tm),
        in_specs=[pl.BlockSpec((1, tm, MLA_Q_RANK), lambda b, i: (b, i, COL_Q // MLA_Q_RANK)),
                  pl.BlockSpec((1, tm, MLA_KV_RANK), lambda b, i: (b, i, COL_KV // MLA_KV_RANK)),
                  pl.BlockSpec((1, tm, LANE), lambda b, i: (b, i, COL_KR // LANE)),
                  pl.BlockSpec((1, tm, 1), lambda b, i: (b, i, 0)),
                  const((1, LANE)), const((1, MLA_Q_RANK)), const((1, MLA_KV_RANK)),
                  const(wq.shape), const(wkv.shape)],
        out_specs=[hspec(MLA_QK), hspec(MLA_QK), hspec(MLA_V)],
        out_shape=[hshape(MLA_QK), hshape(MLA_QK), hshape(MLA_V)],
        compiler_params=_cp("parallel", "parallel"),
        name="mla_prep",
    )(proj, proj, proj, pos, invf, q_g.reshape(1, -1), kv_g.reshape(1, -1), wq, wkv)


def _attn_kernel(q_ref, k_ref, v_ref, o_ref, *, tk):
    q = q_ref[0, 0]
    tq = q.shape[0]
    nk = k_ref.shape[2] // tk

    def body(j, carry):
        m, l, acc = carry
        off = pl.multiple_of(j * tk, tk)
        kj = k_ref[0, 0, pl.ds(off, tk), :]
        vj = v_ref[0, 0, pl.ds(off, tk), :]
        s = lax.dot_general(q, kj, (((1,), (1,)), ((), ())), preferred_element_type=F32)
        m_new = jnp.maximum(m, jnp.max(s, axis=-1, keepdims=True))
        a = jnp.exp2(m - m_new)
        p = jnp.exp2(s - m_new)
        l = a * l + jnp.sum(p, axis=-1, keepdims=True)
        acc = a * acc + jnp.dot(p.astype(BF16), vj, preferred_element_type=F32)
        return m_new, l, acc

    init = (jnp.full((tq, 1), jnp.finfo(F32).min, F32), jnp.zeros((tq, 1), F32),
            jnp.zeros((tq, MLA_V), F32))
    _, l, acc = lax.fori_loop(0, nk, body, init, unroll=True)
    o_ref[0] = (acc / l).astype(o_ref.dtype)


def _attention(q, k, v):
    bsz, nh, seq, _ = q.shape
    tq = min(512, seq)
    tk = min(2048, seq)
    return pl.pallas_call(
        functools.partial(_attn_kernel, tk=tk),
        grid=(bsz, nh, seq // tq),
        in_specs=[pl.BlockSpec((1, 1, tq, MLA_QK), lambda b, h, i: (b, h, i, 0)),
                  pl.BlockSpec((1, 1, seq, MLA_QK), lambda b, h, i: (b, h, 0, 0)),
                  pl.BlockSpec((1, 1, seq, MLA_V), lambda b, h, i: (b, h, 0, 0))],
        out_specs=pl.BlockSpec((1, tq, MLA_V), lambda b, h, i: (b, i, h)),
        out_shape=jax.ShapeDtypeStruct((bsz, seq, nh * MLA_V), BF16),
        compiler_params=_cp("parallel", "parallel", "parallel"),
        name="mla_attention",
    )(q, k, v)


def _s5_params(a_re, a_im, log_step, b_re, b_im, c_re, c_im, d):
    q = S5_CHUNK
    g, p, hh = S5_GROUPS, S5_STATE, S5_GROUP
    j = S5_OCT
    no = g // j
    ks = jnp.arange(q + 1, dtype=F32)[:, None, None]
    out = {}
    state_tabs, out_tabs, c0_tabs = [], [], []

    def oct_rows(w):
        t = w.shape[1]
        return w.reshape(no, j, t, hh, p).transpose(0, 2, 1, 3, 4).reshape(no, t * j * hh, p)

    for direction in range(2):
        ar, ai = a_re[direction].astype(F32), a_im[direction].astype(F32)
        step = jnp.exp(log_step[direction].astype(F32))[:, None]
        mag = jnp.exp(step * ar)
        abar_r = mag * jnp.cos(step * ai)
        abar_i = mag * jnp.sin(step * ai)
        den = ar * ar + ai * ai
        nr = abar_r - 1.0
        fr = (nr * ar + abar_i * ai) / den
        fi = (abar_i * ar - nr * ai) / den
        pmag = jnp.exp(ks * (step * ar)[None])
        pw_r = pmag * jnp.cos(ks * (step * ai)[None])
        pw_i = pmag * jnp.sin(ks * (step * ai)[None])
        br, bi = b_re[direction].astype(F32), b_im[direction].astype(F32)
        bf_r = fr[..., None] * br - fi[..., None] * bi
        bf_i = fr[..., None] * bi + fi[..., None] * br
        cr, ci = c_re[direction].astype(F32), c_im[direction].astype(F32)
        bt_r, bt_i = bf_r.transpose(0, 2, 1)[:, None], bf_i.transpose(0, 2, 1)[:, None]
        tpow = (q - 1 - jnp.arange(q)) if direction == 0 else jnp.arange(q)
        sr = pw_r[tpow].transpose(1, 0, 2)[:, :, None, :]
        si = pw_i[tpow].transpose(1, 0, 2)[:, :, None, :]
        state_tabs += [oct_rows(sr * bt_r - si * bt_i), oct_rows(sr * bt_i + si * bt_r)]
        opow = (jnp.arange(q) + 1) if direction == 0 else (q - jnp.arange(q))
        orr = pw_r[opow].transpose(1, 0, 2)[:, :, None, :]
        oi = pw_i[opow].transpose(1, 0, 2)[:, :, None, :]
        out_tabs += [oct_rows(cr[:, None] * orr - ci[:, None] * oi),
                     oct_rows(-(cr[:, None] * oi + ci[:, None] * orr))]
        c0_tabs += [oct_rows(cr[:, None]), oct_rows(-ci[:, None])]
        out[f'a_r{direction}'] = pw_r[q].reshape(1, g * p)
        out[f'a_i{direction}'] = pw_i[q].reshape(1, g * p)
    out['tab'] = jnp.stack(state_tabs + out_tabs).astype(BF16)
    out['c0'] = jnp.stack(c0_tabs).astype(BF16)
    out['dvec'] = d.astype(F32).reshape(no, 1, j * hh)
    return out


def _expand_const():
    p, j = S5_STATE, S5_OCT
    return jnp.asarray(np.tile(np.eye(p, dtype=np.float32), (1, j)), BF16)


def _expand(block, e_ref, same):
    w = jnp.dot(block, e_ref[...], preferred_element_type=F32)
    return jnp.where(same, w, 0.0).astype(BF16)


def _same_group_mask():
    hh, p, j = S5_GROUP, S5_STATE, S5_OCT
    shape = (j * hh, j * p)
    return (lax.broadcasted_iota(jnp.int32, shape, 0) // hh) == (lax.broadcasted_iota(jnp.int32, shape, 1) // p)


def _s5_state_kernel(x_ref, tab_ref, e_ref, vfr_ref, vfi_ref, vbr_ref, vbi_ref, w_scr):
    q, bw, sw = S5_CHUNK, S5_OCT * S5_GROUP, S5_OCT * S5_STATE

    @pl.when(pl.program_id(1) == 0)
    def _():
        same = _same_group_mask()
        for t in range(q):
            rows = slice(t * bw, (t + 1) * bw)
            for part in range(4):
                w_scr[rows, part * sw:(part + 1) * sw] = _expand(tab_ref[part, 0, rows, :], e_ref, same)

    v = jnp.dot(x_ref[...], w_scr[...], preferred_element_type=F32)
    w = vfr_ref.shape[1]
    vfr_ref[...] = v[:, 0 * w:1 * w]
    vfi_ref[...] = v[:, 1 * w:2 * w]
    vbr_ref[...] = v[:, 2 * w:3 * w]
    vbi_ref[...] = v[:, 3 * w:4 * w]


def _s5_scan_kernel(vfr_ref, vfi_ref, vbr_ref, vbi_ref, afr_ref, afi_ref, abr_ref, abi_ref,
                    sfr_ref, sfi_ref, sbr_ref, sbi_ref):
    nc = vfr_ref.shape[1]
    afr, afi, abr, abi = afr_ref[...], afi_ref[...], abr_ref[...], abi_ref[...]

    def body(c, carry):
        fr, fi, br, bi = carry
        cb = nc - 1 - c
        sfr_ref[0, pl.ds(c, 1), :] = fr
        sfi_ref[0, pl.ds(c, 1), :] = fi
        sbr_ref[0, pl.ds(cb, 1), :] = br
        sbi_ref[0, pl.ds(cb, 1), :] = bi
        nfr = afr * fr - afi * fi + vfr_ref[0, pl.ds(c, 1), :]
        nfi = afr * fi + afi * fr + vfi_ref[0, pl.ds(c, 1), :]
        nbr = abr * br - abi * bi + vbr_ref[0, pl.ds(cb, 1), :]
        nbi = abr * bi + abi * br + vbi_ref[0, pl.ds(cb, 1), :]
        return nfr, nfi, nbr, nbi

    z = jnp.zeros((1, vfr_ref.shape[2]), F32)
    lax.fori_loop(0, nc, body, (z, z, z, z))


def _dot_nt(a, b):
    return lax.dot_general(a, b, (((1,), (1,)), ((), ())), preferred_element_type=F32)


def _s5_out_kernel(x_ref, tabs_ref, tabo_ref, c0_ref, d_ref, sfr_ref, sfi_ref, sbr_ref, sbi_ref, e_ref,
                   y_ref, m_scr, cp_scr):
    q, bw = S5_CHUNK, S5_OCT * S5_GROUP

    @pl.when(pl.program_id(1) == 0)
    def _():
        same = _same_group_mask()
        c0 = [_expand(c0_ref[part, 0], e_ref, same) for part in range(4)]
        lag_f, lag_b = [None] * q, [None] * q
        for t in range(q):
            rows = slice(t * bw, (t + 1) * bw)
            for part in range(4):
                cp_scr[part, rows, :] = _expand(tabo_ref[part, 0, rows, :], e_ref, same)
            ws = [_expand(tabs_ref[part, 0, rows, :], e_ref, same) for part in range(4)]
            lag_f[q - 1 - t] = _dot_nt(ws[0], c0[0]) + _dot_nt(ws[1], c0[1])
            lag_b[t] = _dot_nt(ws[2], c0[2]) + _dot_nt(ws[3], c0[3])
        eye = (lax.broadcasted_iota(jnp.int32, (bw, bw), 0) == lax.broadcasted_iota(jnp.int32, (bw, bw), 1))
        lag0 = (lag_f[0] + lag_b[0] + jnp.where(eye, d_ref[0], 0.0)).astype(BF16)
        lag_f = [v.astype(BF16) for v in lag_f]
        lag_b = [v.astype(BF16) for v in lag_b]
        for ti in range(q):
            for to in range(q):
                blk = lag0 if to == ti else (lag_f[to - ti] if to > ti else lag_b[ti - to])
                m_scr[ti * bw:(ti + 1) * bw, to * bw:(to + 1) * bw] = blk

    y = jnp.dot(x_ref[...], m_scr[...], preferred_element_type=F32)
    for part, s_ref in enumerate((sfr_ref, sfi_ref, sbr_ref, sbi_ref)):
        y += _dot_nt(s_ref[...].astype(BF16), cp_scr[part])
    y_ref[...] = y


def _s5_post_kernel(y_ref, w_ref, b_ref, perm_ref, o_ref):
    oct_w = S5_CHUNK * LANE
    y = jnp.concatenate(
        [jnp.concatenate([y_ref[:, o * oct_w + t * LANE:o * oct_w + (t + 1) * LANE]
                          for o in range(S5_W // LANE)], axis=1)
         for t in range(S5_CHUNK)], axis=0)
    y = jax.nn.gelu(y)
    z = jnp.dot(y.astype(BF16), w_ref[...], preferred_element_type=F32) + b_ref[...]
    o = (y * jax.nn.sigmoid(z)).astype(BF16)
    o_ref[...] = jnp.dot(perm_ref[...], o, preferred_element_type=F32).astype(o_ref.dtype)


def _s5(xs, prm, glu_w, glu_b):
    bsz, nc, _ = xs.shape
    q, g, p = S5_CHUNK, S5_GROUPS, S5_STATE
    seq = nc * q
    no = g // S5_OCT
    rows = bsz * nc
    feat = q * LANE
    sw = S5_OCT * p
    nst = g * p
    x = xs.reshape(rows, no * feat)
    e = _expand_const()
    tab = prm['tab']
    tab_spec = lambda half: pl.BlockSpec((4, 1, feat, p), lambda o, i: (half, o, 0, 0))
    espec = pl.BlockSpec(e.shape, lambda o, i: (0, 0))
    rt = min(512, rows)
    xspec = pl.BlockSpec((rt, feat), lambda o, i: (i, o))
    vspec = pl.BlockSpec((rt, sw), lambda o, i: (i, o))
    vshape = jax.ShapeDtypeStruct((rows, nst), F32)
    v4 = pl.pallas_call(
        _s5_state_kernel,
        grid=(no, rows // rt),
        in_specs=[xspec, tab_spec(0), espec],
        out_specs=[vspec] * 4,
        out_shape=[vshape] * 4,
        scratch_shapes=[pltpu.VMEM((feat, 4 * sw), BF16)],
        compiler_params=_cp("parallel", "arbitrary"),
        name="s5_state",
    )(x, tab, e)

    lt = min(512, nst)
    v4 = [v.reshape(bsz, nc, nst) for v in v4]
    sspec = pl.BlockSpec((1, nc, lt), lambda b, n: (b, 0, n))
    aspec = pl.BlockSpec((1, lt), lambda b, n: (0, n))
    s4 = pl.pallas_call(
        _s5_scan_kernel,
        grid=(bsz, nst // lt),
        in_specs=[sspec] * 4 + [aspec] * 4,
        out_specs=[sspec] * 4,
        out_shape=[jax.ShapeDtypeStruct((bsz, nc, nst), F32)] * 4,
        compiler_params=_cp("parallel", "parallel"),
        name="s5_scan",
    )(*v4, prm['a_r0'], prm['a_i0'], prm['a_r1'], prm['a_i1'])
    s4 = [s.reshape(rows, nst) for s in s4]

    ro = min(256, rows)
    xspec = pl.BlockSpec((ro, feat), lambda o, i: (i, o))
    vspec = pl.BlockSpec((ro, sw), lambda o, i: (i, o))
    bw = S5_OCT * S5_GROUP
    y = pl.pallas_call(
        _s5_out_kernel,
        grid=(no, rows // ro),
        in_specs=([xspec, tab_spec(0), tab_spec(1),
                   pl.BlockSpec((4, 1, bw, p), lambda o, i: (0, o, 0, 0)),
                   pl.BlockSpec((1, 1, bw), lambda o, i: (o, 0, 0))] + [vspec] * 4 + [espec]),
        out_specs=xspec,
        out_shape=jax.ShapeDtypeStruct((rows, no * feat), F32),
        scratch_shapes=[pltpu.VMEM((feat, feat), BF16), pltpu.VMEM((4, feat, sw), BF16)],
        compiler_params=_cp("parallel", "arbitrary"),
        name="s5_out",
    )(x, tab, tab, prm['c0'], prm['dvec'], *s4, e)

    cn = min(32, rows)
    const = lambda shp: pl.BlockSpec(shp, lambda i: (0,) * len(shp))
    out = pl.pallas_call(
        _s5_post_kernel,
        grid=(rows // cn,),
        in_specs=[pl.BlockSpec((cn, no * feat), lambda i: (i, 0)),
                  const((S5_W, S5_W)), const((1, S5_W)), const((cn * q, cn * q))],
        out_specs=pl.BlockSpec((cn * q, S5_W), lambda i: (i, 0)),
        out_shape=jax.ShapeDtypeStruct((rows * q, S5_W), BF16),
        compiler_params=_cp("parallel"),
        name="s5_post",
    )(y, glu_w.astype(BF16), glu_b.reshape(1, -1).astype(F32), _chunk_perm(cn, False))
    return out.reshape(bsz, seq, S5_W)


def _hy_prep_kernel(x0_ref, x1_ref, v_ref, w_ref, b_ref, x0c_ref, gv_ref):
    seq = x0_ref.shape[1]
    row = lax.broadcasted_iota(jnp.int32, (seq, LANE), 0)
    first = row == 0
    last = row == seq - 1

    def conv(ref, k):
        u = ref[0]
        w = w_ref[k]
        prev = jnp.where(first, 0.0, pltpu.roll(u, 1, 0))
        nxt = jnp.where(last, 0.0, pltpu.roll(u, seq - 1, 0))
        return prev * w[0:1] + u * w[1:2] + nxt * w[2:3] + b_ref[k]

    x0c_ref[0] = conv(x0_ref, 0)
    gv_ref[0] = conv(x1_ref, 1) * conv(v_ref, 2)


def _hy_prep(proj, conv_w, conv_b):
    bsz, seq, _ = proj.shape
    nt = HY_W // LANE
    base = COL_HY // LANE
    w = conv_w.astype(F32).reshape(3, 3, HY_W).transpose(1, 0, 2)
    b = conv_b.astype(F32).reshape(3, 1, HY_W)
    col = lambda part: pl.BlockSpec((1, seq, LANE), lambda bb, c: (bb, 0, base + part * nt + c))
    ospec = pl.BlockSpec((1, seq, LANE), lambda bb, c: (bb, 0, c))
    return pl.pallas_call(
        _hy_prep_kernel,
        grid=(bsz, nt),
        in_specs=[col(0), col(1), col(2),
                  pl.BlockSpec((3, 3, LANE), lambda bb, c: (0, 0, c)),
                  pl.BlockSpec((3, 1, LANE), lambda bb, c: (0, 0, c))],
        out_specs=[ospec, ospec],
        out_shape=[jax.ShapeDtypeStruct((bsz, seq, HY_W), F32)] * 2,
        compiler_params=_cp("parallel", "parallel"),
        name="hyena_prep",
    )(proj, proj, proj, w, b)


def _hy_filter_kernel(feat_ref, w1_ref, b1_ref, fq_ref, w2_ref, b2_ref, w3f_ref, w3b_ref,
                      dl_ref, o_ref, h_scr):
    @pl.when(pl.program_id(0) == 0)
    def _():
        fq = fq_ref[...]
        h1 = jnp.sin(fq[0:1] * (jnp.dot(feat_ref[...], w1_ref[...], preferred_element_type=F32) + b1_ref[...]))
        h_scr[...] = jnp.sin(fq[1:2] * (jnp.dot(h1, w2_ref[...], preferred_element_type=F32) + b2_ref[...]))

    h = h_scr[...]
    win = jnp.exp(-feat_ref[:, 0:1] * dl_ref[...])
    hf = jnp.dot(h, w3f_ref[...], preferred_element_type=F32) * win
    hb = jnp.dot(h, w3b_ref[...], preferred_element_type=F32) * win
    row = lax.broadcasted_iota(jnp.int32, hb.shape, 0)
    hb = jnp.where(row == 0, 0.0, hb)
    ss = jnp.sum(hf * hf, axis=0, keepdims=True) + jnp.sum(hb * hb, axis=0, keepdims=True)
    scale = lax.rsqrt(ss + FILTER_EPS)
    o_ref[0] = hf * scale
    o_ref[1] = hb * scale


def _hy_filter(seq, w1, b1, freq, w2, b2, w3):
    t01 = jnp.linspace(0.0, 1.0, seq, dtype=F32)[:, None]
    w = 2.0 * math.pi * jnp.arange(seq, dtype=F32)[:, None] / seq
    bands = jnp.linspace(1e-4, HY_BANDS - 1, HY_BANDS, dtype=F32)[None, :]
    fw = w * bands
    emb = 1 + 2 * HY_BANDS
    kpad = 64
    feat = jnp.concatenate([t01, jnp.cos(fw), -jnp.sin(fw), jnp.zeros((seq, kpad - emb), F32)], axis=-1)
    w1p = jnp.zeros((kpad, w1.shape[1]), F32).at[:emb].set(w1.astype(F32))
    deltas = jnp.abs(jnp.linspace(math.log(HY_DECAY_TARGET) / HY_SLOW_PCT,
                                  math.log(HY_DECAY_TARGET) / HY_FAST_PCT, HY_W, dtype=F32))[None, :]
    nt = HY_W // LANE
    hid = w2.shape[0]
    const = lambda shp: pl.BlockSpec(shp, lambda c: (0,) * len(shp))
    return pl.pallas_call(
        _hy_filter_kernel,
        grid=(nt,),
        in_specs=[const((seq, kpad)), const((kpad, hid)), const((1, hid)), const((2, hid)),
                  const((hid, hid)), const((1, hid)),
                  pl.BlockSpec((hid, LANE), lambda c: (0, c)),
                  pl.BlockSpec((hid, LANE), lambda c: (0, nt + c)),
                  pl.BlockSpec((1, LANE), lambda c: (0, c))],
        out_specs=pl.BlockSpec((2, seq, LANE), lambda c: (0, 0, c)),
        out_shape=jax.ShapeDtypeStruct((2, seq, HY_W), F32),
        scratch_shapes=[pltpu.VMEM((seq, hid), F32)],
        compiler_params=_cp("arbitrary"),
        name="hyena_filter",
    )(feat, w1p, b1.reshape(1, -1).astype(F32), freq.astype(F32), w2.astype(F32),
      b2.reshape(1, -1).astype(F32), w3.astype(F32), w3.astype(F32), deltas)


def _dft_tables(n):
    big = n * n
    kk = np.arange(n)
    ph1 = (2.0 * np.pi / n) * ((kk[:, None] * kk[None, :n // 2]) % n)
    f1 = np.stack([np.cos(ph1), -np.sin(ph1)], axis=1).reshape(2 * n, n // 2)
    eye = np.eye(SUBLANE)
    f4 = jnp.asarray(np.kron(f1.T / big, eye).astype(np.float32), BF16)
    f1 = jnp.asarray(np.kron(f1, eye).astype(np.float32), BF16)
    k1 = jnp.arange(n, dtype=jnp.int32)
    idx = (k1[None, None, :] * (k1[:, None, None] + n * k1[None, :, None])) % big
    th = (2.0 * math.pi / big) * idx.astype(F32)
    gr, gi = jnp.cos(th), -jnp.sin(th)
    gfwd = jnp.concatenate([jnp.concatenate([gr, -gi], axis=2),
                            jnp.concatenate([gi, gr], axis=2)], axis=1)
    hr, hi = jnp.cos(th).transpose(0, 2, 1), jnp.sin(th).transpose(0, 2, 1)
    ginv = jnp.concatenate([jnp.concatenate([hr, -hi], axis=2),
                            jnp.concatenate([hi, hr], axis=2)], axis=1)
    return f1.astype(BF16), f4.astype(BF16), gfwd.astype(BF16), ginv.astype(BF16)


DFT_ROWS = 2 * SUBLANE


def _kron_dot(f_ref, x3):
    k, _, ch = x3.shape
    halves = []
    for h in range(DFT_ROWS // SUBLANE):
        xh = x3[:, h * SUBLANE:(h + 1) * SUBLANE, :].reshape(k * SUBLANE, ch).astype(BF16)
        yh = jnp.dot(f_ref[...], xh, preferred_element_type=F32)
        halves.append(yh.reshape(-1, SUBLANE, ch))
    return jnp.concatenate(halves, axis=1)


def _dft1_kernel(x_ref, f_ref, o_ref):
    o_ref[0, :, 0] = _kron_dot(f_ref, x_ref[0, :, 0]).astype(o_ref.dtype)


def _dft1(x, f1, n):
    bsz, seq, ch = x.shape
    nb = n // DFT_ROWS
    return pl.pallas_call(
        _dft1_kernel,
        grid=(bsz, nb),
        in_specs=[pl.BlockSpec((1, n // 2, 1, DFT_ROWS, ch), lambda b, j: (b, 0, j, 0, 0)),
                  pl.BlockSpec(f1.shape, lambda b, j: (0, 0))],
        out_specs=pl.BlockSpec((1, 2 * n, 1, DFT_ROWS, ch), lambda b, j: (b, 0, j, 0, 0)),
        out_shape=jax.ShapeDtypeStruct((bsz, 2 * n, nb, DFT_ROWS, ch), BF16),
        compiler_params=_cp("parallel", "parallel"),
        name="dft_stage1",
    )(x.reshape(bsz, n // 2, nb, DFT_ROWS, ch), f1)


def _filt_spec_kernel(af_ref, ab_ref, g_ref, hre_ref, him_ref):
    n = g_ref.shape[1] // 2
    for i in range(g_ref.shape[0]):
        xf = jnp.dot(g_ref[i], af_ref[0, i], preferred_element_type=F32)
        xb = jnp.dot(g_ref[i], ab_ref[0, i], preferred_element_type=F32)
        hre_ref[i] = xf[:n] + xb[:n]
        him_ref[i] = xf[n:] - xb[n:]


def _conv_spec_kernel(a_ref, gf_ref, gi_ref, hre_ref, him_ref, c_ref):
    n = gf_ref.shape[1] // 2
    for i in range(gf_ref.shape[0]):
        x = jnp.dot(gf_ref[i], a_ref[0, i], preferred_element_type=F32)
        xr, xi = x[:n], x[n:]
        hr, hi = hre_ref[i], him_ref[i]
        pr = xr * hr - xi * hi
        pi = xr * hi + xi * hr
        pc = jnp.concatenate([pr, pi], axis=0).astype(BF16)
        c_ref[0, i] = jnp.dot(gi_ref[i], pc, preferred_element_type=F32).astype(c_ref.dtype)


def _dft4_kernel(c_ref, f_ref, x0_ref, gv_ref, bias_ref, o_ref):
    y = _kron_dot(f_ref, c_ref[0, :, 0].astype(F32))
    o_ref[0, :, 0] = (x0_ref[0, :, 0] * (y + gv_ref[0, :, 0] * bias_ref[...])).astype(o_ref.dtype)


def _hyena(proj, tables, conv_w, conv_b, f_w1, f_b1, f_freq, f_w2, f_b2, f_w3, f_bias):
    bsz, seq, _ = proj.shape
    n = math.isqrt(2 * seq)
    assert n * n == 2 * seq and n % 2 == 0
    f1, f4, gfwd, ginv = tables
    ch = HY_W
    kb = min(8, n)
    x0c, gv = _hy_prep(proj, conv_w, conv_b)
    filt = _hy_filter(seq, f_w1, f_b1, f_freq, f_w2, f_b2, f_w3)

    a_f = _dft1(filt, f1, n).reshape(2, n, 2 * n, ch)
    adir = lambda bb: pl.BlockSpec((1, kb, 2 * n, ch), lambda k: (bb, k, 0, 0))
    gspec1 = pl.BlockSpec((kb, 2 * n, 2 * n), lambda k: (k, 0, 0))
    hspec1 = pl.BlockSpec((kb, n, ch), lambda k: (k, 0, 0))
    h_re, h_im = pl.pallas_call(
        _filt_spec_kernel,
        grid=(n // kb,),
        in_specs=[adir(0), adir(1), gspec1],
        out_specs=[hspec1, hspec1],
        out_shape=[jax.ShapeDtypeStruct((n, n, ch), F32)] * 2,
        compiler_params=_cp("parallel"),
        name="hyena_filter_spectrum",
    )(a_f, a_f, gfwd)

    a = _dft1(gv, f1, n).reshape(bsz, n, 2 * n, ch)
    aspec = pl.BlockSpec((1, kb, 2 * n, ch), lambda k, b: (b, k, 0, 0))
    gspec = pl.BlockSpec((kb, 2 * n, 2 * n), lambda k, b: (k, 0, 0))
    hspec = pl.BlockSpec((kb, n, ch), lambda k, b: (k, 0, 0))
    c = pl.pallas_call(
        _conv_spec_kernel,
        grid=(n // kb, bsz),
        in_specs=[aspec, gspec, gspec, hspec, hspec],
        out_specs=aspec,
        out_shape=jax.ShapeDtypeStruct((bsz, n, 2 * n, ch), BF16),
        compiler_params=_cp("parallel", "parallel"),
        name="hyena_conv_spectrum",
    )(a, gfwd, ginv, h_re, h_im).reshape(bsz, 2 * n, n // DFT_ROWS, DFT_ROWS, ch)

    nb = n // DFT_ROWS
    nat = lambda arr: arr.reshape(bsz, n // 2, nb, DFT_ROWS, ch)
    rspec = pl.BlockSpec((1, n // 2, 1, DFT_ROWS, ch), lambda b, j: (b, 0, j, 0, 0))
    out = pl.pallas_call(
        _dft4_kernel,
        grid=(bsz, nb),
        in_specs=[pl.BlockSpec((1, 2 * n, 1, DFT_ROWS, ch), lambda b, j: (b, 0, j, 0, 0)),
                  pl.BlockSpec(f4.shape, lambda b, j: (0, 0)),
                  rspec, rspec,
                  pl.BlockSpec((1, ch), lambda b, j: (0, 0))],
        out_specs=rspec,
        out_shape=jax.ShapeDtypeStruct((bsz, n // 2, nb, DFT_ROWS, ch), BF16),
        compiler_params=_cp("parallel", "parallel"),
        name="dft_stage4",
    )(c, f4, nat(x0c), nat(gv), f_bias.astype(F32).reshape(1, ch))
    return out.reshape(bsz, seq, ch)


def _outproj_kernel(om_ref, os_ref, oh_ref, x_ref, mod_ref, w_ref, g_ref, b_ref, o_ref):
    n_m, n_s = om_ref.shape[2], os_ref.shape[2]
    mixed = jnp.dot(om_ref[0], w_ref[0, 0:n_m], preferred_element_type=F32)
    mixed += jnp.dot(os_ref[0], w_ref[0, n_m:n_m + n_s], preferred_element_type=F32)
    mixed += jnp.dot(oh_ref[0], w_ref[0, n_m + n_s:], preferred_element_type=F32)
    gate = mod_ref[0][2:3]
    o_ref[0] = _layer_norm(ALPHA * x_ref[0] + gate * mixed, g_ref[...], b_ref[...])


def _outproj(o_mla, o_s5, o_hy, x, mod, w, l, ln_g, ln_b):
    bsz, seq, d = x.shape
    tm = min(512, seq)
    row = lambda width: pl.BlockSpec((1, tm, width), lambda b, i: (b, i, 0))
    const = lambda shp: pl.BlockSpec(shp, lambda b, i: (0,) * len(shp))
    return pl.pallas_call(
        _outproj_kernel,
        grid=(bsz, seq // tm),
        in_specs=[row(o_mla.shape[2]), row(o_s5.shape[2]), row(o_hy.shape[2]), row(d),
                  pl.BlockSpec((1, 6, d), lambda b, i: (b, 0, 0)),
                  pl.BlockSpec((1,) + w.shape[1:], lambda b, i: (l, 0, 0)), const((1, d)), const((1, d))],
        out_specs=row(d),
        out_shape=jax.ShapeDtypeStruct((bsz, seq, d), F32),
        compiler_params=_cp("parallel", "parallel"),
        name="out_proj_ln",
    )(o_mla, o_s5, o_hy, x, mod, w, ln_g.reshape(1, -1), ln_b.reshape(1, -1))


def _ffn_kernel(xp_ref, x_ref, xn_ref, mod_ref, wg_ref, wu_ref, cw_ref, cb_ref, wd_ref,
                g_ref, b_ref, o_ref, u_scr, acc_scr):
    i, j = pl.program_id(1), pl.program_id(2)
    tm = x_ref.shape[1]
    halo = xp_ref.shape[1]

    @pl.when(j == 0)
    def _():
        m = mod_ref[0]
        sc, sh = 1.0 + m[4:5], m[3:4]
        u_scr[0:halo] = (xp_ref[0] * sc + sh).astype(BF16)
        u_scr[halo:halo + tm] = (x_ref[0] * sc + sh).astype(BF16)
        u_scr[halo + tm:] = (xn_ref[0] * sc + sh).astype(BF16)
        acc_scr[...] = jnp.zeros_like(acc_scr)

    u = u_scr[...]
    rows = tm + 2 * halo
    gx = jnp.dot(u, wg_ref[0], preferred_element_type=F32)
    r = lax.broadcasted_iota(jnp.int32, (tm, 1), 0)
    keep_prev = jnp.logical_or(r > 0, i > 0)
    keep_next = jnp.logical_or(r < tm - 1, i < pl.num_programs(1) - 1)
    g_prev = jnp.where(keep_prev, pltpu.roll(gx, 1, 0)[halo:halo + tm], 0.0)
    g_next = jnp.where(keep_next, pltpu.roll(gx, rows - 1, 0)[halo:halo + tm], 0.0)
    cw = cw_ref[...]
    conv = g_prev * cw[0:1] + gx[halo:halo + tm] * cw[1:2] + g_next * cw[2:3] + cb_ref[...]
    up = jnp.dot(u[halo:halo + tm], wu_ref[0], preferred_element_type=F32)
    h = (conv * jax.nn.sigmoid(conv) * up).astype(BF16)
    acc_scr[...] += jnp.dot(h, wd_ref[0], preferred_element_type=F32)

    @pl.when(j == pl.num_programs(2) - 1)
    def _():
        gate = mod_ref[0][5:6]
        o_ref[0] = _layer_norm(ALPHA * x_ref[0] + gate * acc_scr[...], g_ref[...], b_ref[...])


def _ffn(x, mod, wg, wu, conv_w, conv_b, wd, l, ln_g, ln_b):
    bsz, seq, d = x.shape
    ff = wg.shape[2]
    tm = min(512, seq)
    tn = 512
    halo = SUBLANE
    nhb = seq // halo
    per = tm // halo
    const = lambda shp: pl.BlockSpec(shp, lambda b, i, j: (0,) * len(shp))
    return pl.pallas_call(
        _ffn_kernel,
        grid=(bsz, seq // tm, ff // tn),
        in_specs=[pl.BlockSpec((1, halo, d), lambda b, i, j: (b, jnp.maximum(i * per - 1, 0), 0)),
                  pl.BlockSpec((1, tm, d), lambda b, i, j: (b, i, 0)),
                  pl.BlockSpec((1, halo, d), lambda b, i, j: (b, jnp.minimum((i + 1) * per, nhb - 1), 0)),
                  pl.BlockSpec((1, 6, d), lambda b, i, j: (b, 0, 0)),
                  pl.BlockSpec((1, d, tn), lambda b, i, j: (l, 0, j)),
                  pl.BlockSpec((1, d, tn), lambda b, i, j: (l, 0, j)),
                  pl.BlockSpec((3, tn), lambda b, i, j: (0, j)),
                  pl.BlockSpec((1, tn), lambda b, i, j: (0, j)),
                  pl.BlockSpec((1, tn, d), lambda b, i, j: (l, j, 0)),
                  const((1, d)), const((1, d))],
        out_specs=pl.BlockSpec((1, tm, d), lambda b, i, j: (b, i, 0)),
        out_shape=jax.ShapeDtypeStruct((bsz, seq, d), F32),
        scratch_shapes=[pltpu.VMEM((tm + 2 * halo, d), BF16), pltpu.VMEM((tm, d), F32)],
        compiler_params=_cp("parallel", "parallel", "arbitrary"),
        name="conv_ffn_ln",
    )(x, x, x, mod, wg, wu, conv_w.astype(F32), conv_b.reshape(1, -1).astype(F32), wd,
      ln_g.reshape(1, -1), ln_b.reshape(1, -1))


def _rot_half_cols(w):
    half = w.shape[-1] // 2
    return jnp.concatenate([-w[..., half:], w[..., :half]], axis=-1)


def _prep_w_in(w):
    q, kv, kr, s5, hy = jnp.split(w, (512, 768, 832, 1344), axis=-1)
    return jnp.concatenate([q, hy, kv, kr, _rot_half_cols(kr), s5], axis=-1).astype(BF16)


def _prep_w_uq(w):
    w = w.reshape(w.shape[:-1] + (MLA_HEADS, MLA_QK))
    pe = w[..., MLA_NOPE:]
    return jnp.concatenate([w, _rot_half_cols(pe)], axis=-1).reshape(w.shape[:-2] + (-1,)).astype(BF16)


def kernel(x, c, positions, ada_w, ada_b, w_in, q_norm_g, kv_norm_g, w_uq, w_ukv, s5_a_re, s5_a_im, s5_log_step, s5_b_re, s5_b_im, s5_c_re, s5_c_im, s5_d, s5_glu_w, s5_glu_b, hy_conv_w, hy_conv_b, hy_f_w1, hy_f_b1, hy_f_freq, hy_f_w2, hy_f_b2, hy_f_w3, hy_f_bias, w_out, ln1_g, ln1_b, ffn_w_gate, ffn_w_up, ffn_conv_w, ffn_conv_b, ffn_w_down, ln2_g, ln2_b):
    bsz, seq, _ = x.shape
    depth = ada_w.shape[0]
    mods = _ada(c, ada_w, ada_b)
    tables = _dft_tables(math.isqrt(2 * seq))
    w_in_b = _prep_w_in(w_in)
    w_uq_b = _prep_w_uq(w_uq)
    w_ukv_b = w_ukv.astype(BF16)
    w_out_b = w_out.astype(BF16)
    wg_b, wu_b, wd_b = ffn_w_gate.astype(BF16), ffn_w_up.astype(BF16), ffn_w_down.astype(BF16)
    s5_all = jax.vmap(_s5_params)(s5_a_re, s5_a_im, s5_log_step, s5_b_re, s5_b_im, s5_c_re, s5_c_im, s5_d)
    for l in range(depth):
        mod = mods[l]
        proj, xs5 = _inproj(x, mod, w_in_b, l)
        q, k, v = _mla_prep(proj, positions, q_norm_g[l], kv_norm_g[l], w_uq_b[l], w_ukv_b[l])
        o_mla = _attention(q, k, v)
        o_s5 = _s5(xs5, {name: t[l] for name, t in s5_all.items()}, s5_glu_w[l], s5_glu_b[l])
        o_hy = _hyena(proj, tables, hy_conv_w[l], hy_conv_b[l], hy_f_w1[l], hy_f_b1[l],
                      hy_f_freq[l], hy_f_w2[l], hy_f_b2[l], hy_f_w3[l], hy_f_bias[l])
        x = _outproj(o_mla, o_s5, o_hy, x, mod, w_out_b, l, ln1_g[l], ln1_b[l])
        x = _ffn(x, mod, wg_b, wu_b, ffn_conv_w[l], ffn_conv_b[l], wd_b, l, ln2_g[l], ln2_b[l])
    return x
```

```python
import functools
import math

import numpy as np
import jax
import jax.numpy as jnp
from jax import lax
from jax.experimental import pallas as pl
from jax.experimental.pallas import tpu as pltpu

F32 = jnp.float32
BF16 = jnp.bfloat16

MLA_HEADS = 8
MLA_NOPE = 128
MLA_ROPE = 64
MLA_V = 128
MLA_QK = MLA_NOPE + MLA_ROPE
MLA_Q_RANK = 512
MLA_KV_RANK = 256
ROPE_THETA = 10000.0
S5_W = 512
S5_GROUP = 16
S5_GROUPS = 32
S5_STATE = 64
S5_CHUNK = 16
S5_OCT = 8
HY_W = 512
HY_BANDS = 16
HY_DECAY_TARGET = 1e-2
HY_FAST_PCT = 0.3
HY_SLOW_PCT = 1.5
LN_EPS = 1e-5
RMS_EPS = 1e-6
FILTER_EPS = 1e-6
DEPTH = 2
ALPHA = (2 * DEPTH) ** 0.25

COL_Q = 0
COL_HY = 512
COL_KV = 2048
COL_KR = 2304
COL_S5 = 2432
IN_COLS_PAD = 2944

LANE = 128
SUBLANE = 8
VMEM_LIMIT = 56 * 1024 * 1024


def _cp(*sem):
    return pltpu.CompilerParams(dimension_semantics=sem, vmem_limit_bytes=VMEM_LIMIT)


def _layer_norm(y, g, b):
    mu = jnp.mean(y, axis=-1, keepdims=True)
    d = y - mu
    var = jnp.mean(d * d, axis=-1, keepdims=True)
    return d * lax.rsqrt(var + LN_EPS) * g + b


def _ada_kernel(c_ref, w_ref, b_ref, o_ref):
    c = c_ref[...]
    cond = c * jax.nn.sigmoid(c)
    o_ref[0] = jnp.dot(cond, w_ref[0], preferred_element_type=F32) + b_ref[0]


def _ada(c, ada_w, ada_b):
    bsz, d = c.shape
    depth, _, n = ada_w.shape
    tn = 1024
    cp = jnp.zeros((SUBLANE, d), F32).at[:bsz].set(c)
    out = pl.pallas_call(
        _ada_kernel,
        grid=(depth, n // tn),
        in_specs=[pl.BlockSpec((SUBLANE, d), lambda l, j: (0, 0)),
                  pl.BlockSpec((1, d, tn), lambda l, j: (l, 0, j)),
                  pl.BlockSpec((1, 1, tn), lambda l, j: (l, 0, j))],
        out_specs=pl.BlockSpec((1, SUBLANE, tn), lambda l, j: (l, 0, j)),
        out_shape=jax.ShapeDtypeStruct((depth, SUBLANE, n), F32),
        compiler_params=_cp("parallel", "parallel"),
        name="ada_mod",
    )(cp, ada_w, ada_b.reshape(depth, 1, n))
    return out[:, :bsz].reshape(depth, bsz, 6, d)


def _chunk_perm(chunks, to_time_major):
    q = S5_CHUNK
    nat = np.arange(chunks * q).reshape(chunks, q)
    tmaj = nat.T.reshape(-1)
    p = np.zeros((chunks * q, chunks * q), np.float32)
    if to_time_major:
        p[np.arange(chunks * q), tmaj] = 1.0
    else:
        p[tmaj, np.arange(chunks * q)] = 1.0
    return jnp.asarray(p, BF16)


def _inproj_kernel(x_ref, mod_ref, w_ref, perm_ref, o_ref, xs_ref):
    m = mod_ref[0]
    u = x_ref[0] * (1.0 + m[1:2]) + m[0:1]
    res = jnp.dot(u.astype(BF16), w_ref[0], preferred_element_type=F32)
    o_ref[0] = res[:, :COL_S5]
    us = jnp.dot(perm_ref[...], res[:, COL_S5:].astype(BF16), preferred_element_type=F32).astype(BF16)
    cn = xs_ref.shape[1]
    oct_w = S5_CHUNK * LANE
    for t in range(S5_CHUNK):
        for o in range(S5_W // LANE):
            xs_ref[0, :, o * oct_w + t * LANE:o * oct_w + (t + 1) * LANE] = (
                us[t * cn:(t + 1) * cn, o * LANE:(o + 1) * LANE])


def _inproj(x, mod, w, l):
    bsz, seq, d = x.shape
    n = w.shape[2]
    tm = min(256, seq)
    cn = tm // S5_CHUNK
    return pl.pallas_call(
        _inproj_kernel,
        grid=(bsz, seq // tm),
        in_specs=[pl.BlockSpec((1, tm, d), lambda b, i: (b, i, 0)),
                  pl.BlockSpec((1, 6, d), lambda b, i: (b, 0, 0)),
                  pl.BlockSpec((1, d, n), lambda b, i: (l, 0, 0)),
                  pl.BlockSpec((tm, tm), lambda b, i: (0, 0))],
        out_specs=[pl.BlockSpec((1, tm, COL_S5), lambda b, i: (b, i, 0)),
                   pl.BlockSpec((1, cn, S5_CHUNK * S5_W), lambda b, i: (b, i, 0))],
        out_shape=[jax.ShapeDtypeStruct((bsz, seq, COL_S5), F32),
                   jax.ShapeDtypeStruct((bsz, seq // S5_CHUNK, S5_CHUNK * S5_W), BF16)],
        compiler_params=_cp("parallel", "parallel"),
        name="in_proj",
    )(x, mod, w, _chunk_perm(cn, True))


def _mla_prep_kernel(ql_ref, kvl_ref, kr_ref, pos_ref, invf_ref, qg_ref, kvg_ref,
                     wq_ref, wkv_ref, q_ref, k_ref, v_ref):
    scale = MLA_QK ** -0.5 * math.log2(math.e)
    ql = ql_ref[0]
    qn = ql * lax.rsqrt(jnp.mean(ql * ql, axis=-1, keepdims=True) + RMS_EPS) * qg_ref[...]
    kvl = kvl_ref[0]
    kvn = kvl * lax.rsqrt(jnp.mean(kvl * kvl, axis=-1, keepdims=True) + RMS_EPS) * kvg_ref[...]
    ang = pos_ref[0] * invf_ref[...]
    cos = jnp.cos(ang)
    sin = jnp.sin(ang)
    qa = jnp.dot(qn.astype(BF16), wq_ref[...], preferred_element_type=F32)
    kva = jnp.dot(kvn.astype(BF16), wkv_ref[...], preferred_element_type=F32)
    kr = kr_ref[0]
    kpe = (kr * cos + pltpu.roll(kr, MLA_ROPE, 1) * sin)[:, :MLA_ROPE].astype(BF16)
    for h in range(MLA_HEADS):
        c0 = 2 * LANE * h
        pr = qa[:, c0 + LANE:c0 + 2 * LANE]
        qpe = pr * cos + pltpu.roll(pr, MLA_ROPE, 1) * sin
        q_ref[0, h, :, 0:MLA_NOPE] = (qa[:, c0:c0 + LANE] * scale).astype(BF16)
        q_ref[0, h, :, MLA_NOPE:MLA_QK] = (qpe[:, :MLA_ROPE] * scale).astype(BF16)
        k_ref[0, h, :, 0:MLA_NOPE] = kva[:, c0:c0 + LANE].astype(BF16)
        k_ref[0, h, :, MLA_NOPE:MLA_QK] = kpe
        v_ref[0, h] = kva[:, c0 + LANE:c0 + 2 * LANE].astype(BF16)


def _mla_prep(proj, positions, q_g, kv_g, wq, wkv):
    bsz, seq, _ = proj.shape
    tm = min(512, seq)
    inv_freq = ROPE_THETA ** (-jnp.arange(0, MLA_ROPE, 2, dtype=F32) / MLA_ROPE)
    invf = jnp.tile(inv_freq, LANE // (MLA_ROPE // 2)).reshape(1, LANE)
    pos = positions.astype(F32).reshape(bsz, seq, 1)
    hshape = lambda w: jax.ShapeDtypeStruct((bsz, MLA_HEADS, seq, w), BF16)
    hspec = lambda w: pl.BlockSpec((1, MLA_HEADS, tm, w), lambda b, i: (b, 0, i, 0))
    const = lambda shp: pl.BlockSpec(shp, lambda b, i: (0,) * len(shp))
    return pl.pallas_call(
        _mla_prep_kernel,
        grid=(bsz, seq // tm),
        in_specs=[pl.BlockSpec((1, tm, MLA_Q_RANK), lambda b, i: (b, i, COL_Q // MLA_Q_RANK)),
                  pl.BlockSpec((1, tm, MLA_KV_RANK), lambda b, i: (b, i, COL_KV // MLA_KV_RANK)),
                  pl.BlockSpec((1, tm, LANE), lambda b, i: (b, i, COL_KR // LANE)),
                  pl.BlockSpec((1, tm, 1), lambda b, i: (b, i, 0)),
                  const((1, LANE)), const((1, MLA_Q_RANK)), const((1, MLA_KV_RANK)),
                  const(wq.shape), const(wkv.shape)],
        out_specs=[hspec(MLA_QK), hspec(MLA_QK), hspec(MLA_V)],
        out_shape=[hshape(MLA_QK), hshape(MLA_QK), hshape(MLA_V)],
        compiler_params=_cp("parallel", "parallel"),
        name="mla_prep",
    )(proj, proj, proj, pos, invf, q_g.reshape(1, -1), kv_g.reshape(1, -1), wq, wkv)


def _attn_kernel(q_ref, k_ref, v_ref, o_ref, *, tk):
    q = q_ref[0, 0]
    tq = q.shape[0]
    nk = k_ref.shape[2] // tk

    def body(j, carry):
        m, l, acc = carry
        off = pl.multiple_of(j * tk, tk)
        kj = k_ref[0, 0, pl.ds(off, tk), :]
        vj = v_ref[0, 0, pl.ds(off, tk), :]
        s = lax.dot_general(q, kj, (((1,), (1,)), ((), ())), preferred_element_type=F32)
        m_new = jnp.maximum(m, jnp.max(s, axis=-1, keepdims=True))
        a = jnp.exp2(m - m_new)
        p = jnp.exp2(s - m_new)
        l = a * l + jnp.sum(p, axis=-1, keepdims=True)
        acc = a * acc + jnp.dot(p.astype(BF16), vj, preferred_element_type=F32)
        return m_new, l, acc

    init = (jnp.full((tq, 1), jnp.finfo(F32).min, F32), jnp.zeros((tq, 1), F32),
            jnp.zeros((tq, MLA_V), F32))
    _, l, acc = lax.fori_loop(0, nk, body, init, unroll=True)
    o_ref[0] = (acc / l).astype(o_ref.dtype)


def _attention(q, k, v):
    bsz, nh, seq, _ = q.shape
    tq = min(1024, seq)
    tk = min(2048, seq)
    return pl.pallas_call(
        functools.partial(_attn_kernel, tk=tk),
        grid=(bsz, nh, seq // tq),
        in_specs=[pl.BlockSpec((1, 1, tq, MLA_QK), lambda b, h, i: (b, h, i, 0)),
                  pl.BlockSpec((1, 1, seq, MLA_QK), lambda b, h, i: (b, h, 0, 0)),
                  pl.BlockSpec((1, 1, seq, MLA_V), lambda b, h, i: (b, h, 0, 0))],
        out_specs=pl.BlockSpec((1, tq, MLA_V), lambda b, h, i: (b, i, h)),
        out_shape=jax.ShapeDtypeStruct((bsz, seq, nh * MLA_V), BF16),
        compiler_params=_cp("parallel", "parallel", "parallel"),
        name="mla_attention",
    )(q, k, v)


def _s5_params(a_re, a_im, log_step, b_re, b_im, c_re, c_im, d):
    q = S5_CHUNK
    g, p, hh = S5_GROUPS, S5_STATE, S5_GROUP
    j = S5_OCT
    no = g // j
    ks = jnp.arange(q + 1, dtype=F32)[:, None, None]
    out = {}
    state_tabs, out_tabs, c0_tabs = [], [], []

    def oct_rows(w):
        t = w.shape[1]
        return w.reshape(no, j, t, hh, p).transpose(0, 2, 1, 3, 4).reshape(no, t * j * hh, p)

    for direction in range(2):
        ar, ai = a_re[direction].astype(F32), a_im[direction].astype(F32)
        step = jnp.exp(log_step[direction].astype(F32))[:, None]
        mag = jnp.exp(step * ar)
        abar_r = mag * jnp.cos(step * ai)
        abar_i = mag * jnp.sin(step * ai)
        den = ar * ar + ai * ai
        nr = abar_r - 1.0
        fr = (nr * ar + abar_i * ai) / den
        fi = (abar_i * ar - nr * ai) / den
        pmag = jnp.exp(ks * (step * ar)[None])
        pw_r = pmag * jnp.cos(ks * (step * ai)[None])
        pw_i = pmag * jnp.sin(ks * (step * ai)[None])
        br, bi = b_re[direction].astype(F32), b_im[direction].astype(F32)
        bf_r = fr[..., None] * br - fi[..., None] * bi
        bf_i = fr[..., None] * bi + fi[..., None] * br
        cr, ci = c_re[direction].astype(F32), c_im[direction].astype(F32)
        bt_r, bt_i = bf_r.transpose(0, 2, 1)[:, None], bf_i.transpose(0, 2, 1)[:, None]
        tpow = (q - 1 - jnp.arange(q)) if direction == 0 else jnp.arange(q)
        sr = pw_r[tpow].transpose(1, 0, 2)[:, :, None, :]
        si = pw_i[tpow].transpose(1, 0, 2)[:, :, None, :]
        state_tabs += [oct_rows(sr * bt_r - si * bt_i), oct_rows(sr * bt_i + si * bt_r)]
        opow = (jnp.arange(q) + 1) if direction == 0 else (q - jnp.arange(q))
        orr = pw_r[opow].transpose(1, 0, 2)[:, :, None, :]
        oi = pw_i[opow].transpose(1, 0, 2)[:, :, None, :]
        out_tabs += [oct_rows(cr[:, None] * orr - ci[:, None] * oi),
                     oct_rows(-(cr[:, None] * oi + ci[:, None] * orr))]
        c0_tabs += [oct_rows(cr[:, None]), oct_rows(-ci[:, None])]
        out[f'a_r{direction}'] = pw_r[q].reshape(1, g * p)
        out[f'a_i{direction}'] = pw_i[q].reshape(1, g * p)
    out['tab'] = jnp.stack(state_tabs + out_tabs).astype(BF16)
    out['c0'] = jnp.stack(c0_tabs).astype(BF16)
    out['dvec'] = d.astype(F32).reshape(no, 1, j * hh)
    return out


def _expand_const():
    p, j = S5_STATE, S5_OCT
    return jnp.asarray(np.tile(np.eye(p, dtype=np.float32), (1, j)), BF16)


def _expand(block, e_ref, same):
    w = jnp.dot(block, e_ref[...], preferred_element_type=F32)
    return jnp.where(same, w, 0.0).astype(BF16)


def _same_group_mask():
    hh, p, j = S5_GROUP, S5_STATE, S5_OCT
    shape = (j * hh, j * p)
    return (lax.broadcasted_iota(jnp.int32, shape, 0) // hh) == (lax.broadcasted_iota(jnp.int32, shape, 1) // p)


def _s5_state_kernel(x_ref, tab_ref, e_ref, vfr_ref, vfi_ref, vbr_ref, vbi_ref, w_scr):
    q, bw, sw = S5_CHUNK, S5_OCT * S5_GROUP, S5_OCT * S5_STATE

    @pl.when(pl.program_id(1) == 0)
    def _():
        same = _same_group_mask()
        for t in range(q):
            rows = slice(t * bw, (t + 1) * bw)
            for part in range(4):
                w_scr[rows, part * sw:(part + 1) * sw] = _expand(tab_ref[0, part, 0, rows, :], e_ref, same)

    v = jnp.dot(x_ref[...], w_scr[...], preferred_element_type=F32)
    w = vfr_ref.shape[1]
    vfr_ref[...] = v[:, 0 * w:1 * w]
    vfi_ref[...] = v[:, 1 * w:2 * w]
    vbr_ref[...] = v[:, 2 * w:3 * w]
    vbi_ref[...] = v[:, 3 * w:4 * w]


def _s5_scan_kernel(vfr_ref, vfi_ref, vbr_ref, vbi_ref, afr_ref, afi_ref, abr_ref, abi_ref,
                    sfr_ref, sfi_ref, sbr_ref, sbi_ref):
    nc = vfr_ref.shape[1]
    afr, afi, abr, abi = afr_ref[0], afi_ref[0], abr_ref[0], abi_ref[0]

    def body(c, carry):
        fr, fi, br, bi = carry
        cb = nc - 1 - c
        sfr_ref[0, pl.ds(c, 1), :] = fr
        sfi_ref[0, pl.ds(c, 1), :] = fi
        sbr_ref[0, pl.ds(cb, 1), :] = br
        sbi_ref[0, pl.ds(cb, 1), :] = bi
        nfr = afr * fr - afi * fi + vfr_ref[0, pl.ds(c, 1), :]
        nfi = afr * fi + afi * fr + vfi_ref[0, pl.ds(c, 1), :]
        nbr = abr * br - abi * bi + vbr_ref[0, pl.ds(cb, 1), :]
        nbi = abr * bi + abi * br + vbi_ref[0, pl.ds(cb, 1), :]
        return nfr, nfi, nbr, nbi

    z = jnp.zeros((1, vfr_ref.shape[2]), F32)
    lax.fori_loop(0, nc, body, (z, z, z, z))


def _dot_nt(a, b):
    return lax.dot_general(a, b, (((1,), (1,)), ((), ())), preferred_element_type=F32)


def _s5_out_kernel(x_ref, tabs_ref, tabo_ref, c0_ref, d_ref, sfr_ref, sfi_ref, sbr_ref, sbi_ref, e_ref,
                   y_ref, m_scr, cp_scr):
    q, bw = S5_CHUNK, S5_OCT * S5_GROUP

    @pl.when(pl.program_id(1) == 0)
    def _():
        same = _same_group_mask()
        c0 = [_expand(c0_ref[0, part, 0], e_ref, same) for part in range(4)]
        lag_f, lag_b = [None] * q, [None] * q
        for t in range(q):
            rows = slice(t * bw, (t + 1) * bw)
            for part in range(4):
                cp_scr[part, rows, :] = _expand(tabo_ref[0, part, 0, rows, :], e_ref, same)
            ws = [_expand(tabs_ref[0, part, 0, rows, :], e_ref, same) for part in range(4)]
            lag_f[q - 1 - t] = _dot_nt(ws[0], c0[0]) + _dot_nt(ws[1], c0[1])
            lag_b[t] = _dot_nt(ws[2], c0[2]) + _dot_nt(ws[3], c0[3])
        eye = (lax.broadcasted_iota(jnp.int32, (bw, bw), 0) == lax.broadcasted_iota(jnp.int32, (bw, bw), 1))
        lag0 = (lag_f[0] + lag_b[0] + jnp.where(eye, d_ref[0, 0], 0.0)).astype(BF16)
        lag_f = [v.astype(BF16) for v in lag_f]
        lag_b = [v.astype(BF16) for v in lag_b]
        for ti in range(q):
            for to in range(q):
                blk = lag0 if to == ti else (lag_f[to - ti] if to > ti else lag_b[ti - to])
                m_scr[ti * bw:(ti + 1) * bw, to * bw:(to + 1) * bw] = blk

    y = jnp.dot(x_ref[...], m_scr[...], preferred_element_type=F32)
    for part, s_ref in enumerate((sfr_ref, sfi_ref, sbr_ref, sbi_ref)):
        y += _dot_nt(s_ref[...].astype(BF16), cp_scr[part])
    y_ref[...] = y


def _s5_post_kernel(y_ref, w_ref, b_ref, perm_ref, o_ref):
    oct_w = S5_CHUNK * LANE
    y = jnp.concatenate(
        [jnp.concatenate([y_ref[:, o * oct_w + t * LANE:o * oct_w + (t + 1) * LANE]
                          for o in range(S5_W // LANE)], axis=1)
         for t in range(S5_CHUNK)], axis=0)
    y = jax.nn.gelu(y)
    z = jnp.dot(y.astype(BF16), w_ref[...], preferred_element_type=F32) + b_ref[...]
    o = (y * jax.nn.sigmoid(z)).astype(BF16)
    o_ref[...] = jnp.dot(perm_ref[...], o, preferred_element_type=F32).astype(o_ref.dtype)


def _s5(xs, prm, l, glu_w, glu_b):
    bsz, nc, _ = xs.shape
    q, g, p = S5_CHUNK, S5_GROUPS, S5_STATE
    seq = nc * q
    no = g // S5_OCT
    rows = bsz * nc
    feat = q * LANE
    sw = S5_OCT * p
    nst = g * p
    x = xs.reshape(rows, no * feat)
    e = _expand_const()
    tab = prm['tab']
    tab_spec = lambda half: pl.BlockSpec((1, 4, 1, feat, p), lambda o, i: (l, half, o, 0, 0))
    espec = pl.BlockSpec(e.shape, lambda o, i: (0, 0))
    rt = min(512, rows)
    xspec = pl.BlockSpec((rt, feat), lambda o, i: (i, o))
    vspec = pl.BlockSpec((rt, sw), lambda o, i: (i, o))
    vshape = jax.ShapeDtypeStruct((rows, nst), F32)
    v4 = pl.pallas_call(
        _s5_state_kernel,
        grid=(no, rows // rt),
        in_specs=[xspec, tab_spec(0), espec],
        out_specs=[vspec] * 4,
        out_shape=[vshape] * 4,
        scratch_shapes=[pltpu.VMEM((feat, 4 * sw), BF16)],
        compiler_params=_cp("parallel", "arbitrary"),
        name="s5_state",
    )(x, tab, e)

    lt = min(512, nst)
    v4 = [v.reshape(bsz, nc, nst) for v in v4]
    sspec = pl.BlockSpec((1, nc, lt), lambda b, n: (b, 0, n))
    aspec = pl.BlockSpec((1, 1, lt), lambda b, n: (l, 0, n))
    s4 = pl.pallas_call(
        _s5_scan_kernel,
        grid=(bsz, nst // lt),
        in_specs=[sspec] * 4 + [aspec] * 4,
        out_specs=[sspec] * 4,
        out_shape=[jax.ShapeDtypeStruct((bsz, nc, nst), F32)] * 4,
        compiler_params=_cp("parallel", "parallel"),
        name="s5_scan",
    )(*v4, prm['a_r0'], prm['a_i0'], prm['a_r1'], prm['a_i1'])
    s4 = [s.reshape(rows, nst) for s in s4]

    ro = min(256, rows)
    xspec = pl.BlockSpec((ro, feat), lambda o, i: (i, o))
    vspec = pl.BlockSpec((ro, sw), lambda o, i: (i, o))
    bw = S5_OCT * S5_GROUP
    y = pl.pallas_call(
        _s5_out_kernel,
        grid=(no, rows // ro),
        in_specs=([xspec, tab_spec(0), tab_spec(1),
                   pl.BlockSpec((1, 4, 1, bw, p), lambda o, i: (l, 0, o, 0, 0)),
                   pl.BlockSpec((1, 1, 1, bw), lambda o, i: (l, o, 0, 0))] + [vspec] * 4 + [espec]),
        out_specs=xspec,
        out_shape=jax.ShapeDtypeStruct((rows, no * feat), F32),
        scratch_shapes=[pltpu.VMEM((feat, feat), BF16), pltpu.VMEM((4, feat, sw), BF16)],
        compiler_params=_cp("parallel", "arbitrary"),
        name="s5_out",
    )(x, tab, tab, prm['c0'], prm['dvec'], *s4, e)

    cn = min(32, rows)
    const = lambda shp: pl.BlockSpec(shp, lambda i: (0,) * len(shp))
    out = pl.pallas_call(
        _s5_post_kernel,
        grid=(rows // cn,),
        in_specs=[pl.BlockSpec((cn, no * feat), lambda i: (i, 0)),
                  const((S5_W, S5_W)), const((1, S5_W)), const((cn * q, cn * q))],
        out_specs=pl.BlockSpec((cn * q, S5_W), lambda i: (i, 0)),
        out_shape=jax.ShapeDtypeStruct((rows * q, S5_W), BF16),
        compiler_params=_cp("parallel"),
        name="s5_post",
    )(y, glu_w.astype(BF16), glu_b.reshape(1, -1).astype(F32), _chunk_perm(cn, False))
    return out.reshape(bsz, seq, S5_W)


def _hy_prep_kernel(x0_ref, x1_ref, v_ref, w_ref, b_ref, x0c_ref, gv_ref):
    seq = x0_ref.shape[1]
    row = lax.broadcasted_iota(jnp.int32, (seq, LANE), 0)
    first = row == 0
    last = row == seq - 1

    def conv(ref, k):
        u = ref[0]
        w = w_ref[k]
        prev = jnp.where(first, 0.0, pltpu.roll(u, 1, 0))
        nxt = jnp.where(last, 0.0, pltpu.roll(u, seq - 1, 0))
        return prev * w[0:1] + u * w[1:2] + nxt * w[2:3] + b_ref[k]

    x0c_ref[0] = conv(x0_ref, 0)
    gv_ref[0] = conv(x1_ref, 1) * conv(v_ref, 2)


def _hy_prep(proj, conv_w, conv_b):
    bsz, seq, _ = proj.shape
    nt = HY_W // LANE
    base = COL_HY // LANE
    w = conv_w.astype(F32).reshape(3, 3, HY_W).transpose(1, 0, 2)
    b = conv_b.astype(F32).reshape(3, 1, HY_W)
    col = lambda part: pl.BlockSpec((1, seq, LANE), lambda bb, c: (bb, 0, base + part * nt + c))
    ospec = pl.BlockSpec((1, seq, LANE), lambda bb, c: (bb, 0, c))
    return pl.pallas_call(
        _hy_prep_kernel,
        grid=(bsz, nt),
        in_specs=[col(0), col(1), col(2),
                  pl.BlockSpec((3, 3, LANE), lambda bb, c: (0, 0, c)),
                  pl.BlockSpec((3, 1, LANE), lambda bb, c: (0, 0, c))],
        out_specs=[ospec, ospec],
        out_shape=[jax.ShapeDtypeStruct((bsz, seq, HY_W), F32)] * 2,
        compiler_params=_cp("parallel", "parallel"),
        name="hyena_prep",
    )(proj, proj, proj, w, b)


def _hy_filter_kernel(feat_ref, w1_ref, b1_ref, fq_ref, w2_ref, b2_ref, w3f_ref, w3b_ref,
                      dl_ref, o_ref, h_scr):
    @pl.when(pl.program_id(0) == 0)
    def _():
        fq = fq_ref[...]
        h1 = jnp.sin(fq[0:1] * (jnp.dot(feat_ref[...], w1_ref[...], preferred_element_type=F32) + b1_ref[...]))
        h_scr[...] = jnp.sin(fq[1:2] * (jnp.dot(h1, w2_ref[...], preferred_element_type=F32) + b2_ref[...]))

    h = h_scr[...]
    win = jnp.exp(-feat_ref[:, 0:1] * dl_ref[...])
    hf = jnp.dot(h, w3f_ref[...], preferred_element_type=F32) * win
    hb = jnp.dot(h, w3b_ref[...], preferred_element_type=F32) * win
    row = lax.broadcasted_iota(jnp.int32, hb.shape, 0)
    hb = jnp.where(row == 0, 0.0, hb)
    ss = jnp.sum(hf * hf, axis=0, keepdims=True) + jnp.sum(hb * hb, axis=0, keepdims=True)
    scale = lax.rsqrt(ss + FILTER_EPS)
    o_ref[0] = hf * scale
    o_ref[1] = hb * scale


def _hy_filter(seq, w1, b1, freq, w2, b2, w3):
    t01 = jnp.linspace(0.0, 1.0, seq, dtype=F32)[:, None]
    w = 2.0 * math.pi * jnp.arange(seq, dtype=F32)[:, None] / seq
    bands = jnp.linspace(1e-4, HY_BANDS - 1, HY_BANDS, dtype=F32)[None, :]
    fw = w * bands
    emb = 1 + 2 * HY_BANDS
    kpad = 64
    feat = jnp.concatenate([t01, jnp.cos(fw), -jnp.sin(fw), jnp.zeros((seq, kpad - emb), F32)], axis=-1)
    w1p = jnp.zeros((kpad, w1.shape[1]), F32).at[:emb].set(w1.astype(F32))
    deltas = jnp.abs(jnp.linspace(math.log(HY_DECAY_TARGET) / HY_SLOW_PCT,
                                  math.log(HY_DECAY_TARGET) / HY_FAST_PCT, HY_W, dtype=F32))[None, :]
    nt = HY_W // LANE
    hid = w2.shape[0]
    const = lambda shp: pl.BlockSpec(shp, lambda c: (0,) * len(shp))
    return pl.pallas_call(
        _hy_filter_kernel,
        grid=(nt,),
        in_specs=[const((seq, kpad)), const((kpad, hid)), const((1, hid)), const((2, hid)),
                  const((hid, hid)), const((1, hid)),
                  pl.BlockSpec((hid, LANE), lambda c: (0, c)),
                  pl.BlockSpec((hid, LANE), lambda c: (0, nt + c)),
                  pl.BlockSpec((1, LANE), lambda c: (0, c))],
        out_specs=pl.BlockSpec((2, seq, LANE), lambda c: (0, 0, c)),
        out_shape=jax.ShapeDtypeStruct((2, seq, HY_W), F32),
        scratch_shapes=[pltpu.VMEM((seq, hid), F32)],
        compiler_params=_cp("arbitrary"),
        name="hyena_filter",
    )(feat, w1p, b1.reshape(1, -1).astype(F32), freq.astype(F32), w2.astype(F32),
      b2.reshape(1, -1).astype(F32), w3.astype(F32), w3.astype(F32), deltas)


def _dft_tables(n):
    big = n * n
    kk = np.arange(n)
    ph1 = (2.0 * np.pi / n) * ((kk[:, None] * kk[None, :n // 2]) % n)
    f1 = np.stack([np.cos(ph1), -np.sin(ph1)], axis=1).reshape(2 * n, n // 2)
    eye = np.eye(SUBLANE)
    f4 = jnp.asarray(np.kron(f1.T / big, eye).astype(np.float32), BF16)
    f1 = jnp.asarray(np.kron(f1, eye).astype(np.float32), BF16)
    idx = (kk[None, None, :] * (kk[:, None, None] + n * kk[None, :, None])) % big
    th = (2.0 * np.pi / big) * idx
    gr, gi = np.cos(th), -np.sin(th)
    gfwd = np.concatenate([np.concatenate([gr, -gi], axis=2),
                           np.concatenate([gi, gr], axis=2)], axis=1)
    return f1, f4, jnp.asarray(gfwd.astype(np.float32), BF16)


DFT_ROWS = 2 * SUBLANE


def _kron_dot(f_ref, x3):
    k, _, ch = x3.shape
    halves = []
    for h in range(DFT_ROWS // SUBLANE):
        xh = x3[:, h * SUBLANE:(h + 1) * SUBLANE, :].reshape(k * SUBLANE, ch).astype(BF16)
        yh = jnp.dot(f_ref[...], xh, preferred_element_type=F32)
        halves.append(yh.reshape(-1, SUBLANE, ch))
    return jnp.concatenate(halves, axis=1)


def _dft1_kernel(x_ref, f_ref, o_ref):
    o_ref[0, :, 0] = _kron_dot(f_ref, x_ref[0, :, 0]).astype(o_ref.dtype)


def _dft1(x, f1, n):
    bsz, seq, ch = x.shape
    nb = n // DFT_ROWS
    return pl.pallas_call(
        _dft1_kernel,
        grid=(bsz, nb),
        in_specs=[pl.BlockSpec((1, n // 2, 1, DFT_ROWS, ch), lambda b, j: (b, 0, j, 0, 0)),
                  pl.BlockSpec(f1.shape, lambda b, j: (0, 0))],
        out_specs=pl.BlockSpec((1, 2 * n, 1, DFT_ROWS, ch), lambda b, j: (b, 0, j, 0, 0)),
        out_shape=jax.ShapeDtypeStruct((bsz, 2 * n, nb, DFT_ROWS, ch), BF16),
        compiler_params=_cp("parallel", "parallel"),
        name="dft_stage1",
    )(x.reshape(bsz, n // 2, nb, DFT_ROWS, ch), f1)


def _filt_spec_kernel(af_ref, ab_ref, g_ref, hre_ref, him_ref):
    n = g_ref.shape[1] // 2
    for i in range(g_ref.shape[0]):
        xf = jnp.dot(g_ref[i], af_ref[0, i], preferred_element_type=F32)
        xb = jnp.dot(g_ref[i], ab_ref[0, i], preferred_element_type=F32)
        hre_ref[i] = xf[:n] + xb[:n]
        him_ref[i] = xf[n:] - xb[n:]


def _conv_spec_kernel(a_ref, gf_ref, hre_ref, him_ref, c_ref):
    n = gf_ref.shape[1] // 2
    for i in range(gf_ref.shape[0]):
        x = jnp.dot(gf_ref[i], a_ref[0, i], preferred_element_type=F32)
        xr, xi = x[:n], x[n:]
        hr, hi = hre_ref[i], him_ref[i]
        pr = xr * hr - xi * hi
        pi = xr * hi + xi * hr
        pc = jnp.concatenate([pr, pi], axis=0).astype(BF16)
        c = lax.dot_general(gf_ref[i], pc, (((0,), (0,)), ((), ())), preferred_element_type=F32)
        c_ref[0, i] = c.astype(c_ref.dtype)


def _dft4_kernel(c_ref, f_ref, x0_ref, gv_ref, bias_ref, o_ref):
    y = _kron_dot(f_ref, c_ref[0, :, 0].astype(F32))
    o_ref[0, :, 0] = (x0_ref[0, :, 0] * (y + gv_ref[0, :, 0] * bias_ref[...])).astype(o_ref.dtype)


def _hyena(proj, tables, conv_w, conv_b, f_w1, f_b1, f_freq, f_w2, f_b2, f_w3, f_bias):
    bsz, seq, _ = proj.shape
    n = math.isqrt(2 * seq)
    assert n * n == 2 * seq and n % 2 == 0
    f1, f4, gfwd = tables
    ch = HY_W
    kb = min(8, n)
    x0c, gv = _hy_prep(proj, conv_w, conv_b)
    filt = _hy_filter(seq, f_w1, f_b1, f_freq, f_w2, f_b2, f_w3)

    a_f = _dft1(filt, f1, n).reshape(2, n, 2 * n, ch)
    adir = lambda bb: pl.BlockSpec((1, kb, 2 * n, ch), lambda k: (bb, k, 0, 0))
    gspec1 = pl.BlockSpec((kb, 2 * n, 2 * n), lambda k: (k, 0, 0))
    hspec1 = pl.BlockSpec((kb, n, ch), lambda k: (k, 0, 0))
    h_re, h_im = pl.pallas_call(
        _filt_spec_kernel,
        grid=(n // kb,),
        in_specs=[adir(0), adir(1), gspec1],
        out_specs=[hspec1, hspec1],
        out_shape=[jax.ShapeDtypeStruct((n, n, ch), F32)] * 2,
        compiler_params=_cp("parallel"),
        name="hyena_filter_spectrum",
    )(a_f, a_f, gfwd)

    a = _dft1(gv, f1, n).reshape(bsz, n, 2 * n, ch)
    aspec = pl.BlockSpec((1, kb, 2 * n, ch), lambda k, b: (b, k, 0, 0))
    gspec = pl.BlockSpec((kb, 2 * n, 2 * n), lambda k, b: (k, 0, 0))
    hspec = pl.BlockSpec((kb, n, ch), lambda k, b: (k, 0, 0))
    c = pl.pallas_call(
        _conv_spec_kernel,
        grid=(n // kb, bsz),
        in_specs=[aspec, gspec, hspec, hspec],
        out_specs=aspec,
        out_shape=jax.ShapeDtypeStruct((bsz, n, 2 * n, ch), BF16),
        compiler_params=_cp("parallel", "parallel"),
        name="hyena_conv_spectrum",
    )(a, gfwd, h_re, h_im).reshape(bsz, 2 * n, n // DFT_ROWS, DFT_ROWS, ch)

    nb = n // DFT_ROWS
    nat = lambda arr: arr.reshape(bsz, n // 2, nb, DFT_ROWS, ch)
    rspec = pl.BlockSpec((1, n // 2, 1, DFT_ROWS, ch), lambda b, j: (b, 0, j, 0, 0))
    out = pl.pallas_call(
        _dft4_kernel,
        grid=(bsz, nb),
        in_specs=[pl.BlockSpec((1, 2 * n, 1, DFT_ROWS, ch), lambda b, j: (b, 0, j, 0, 0)),
                  pl.BlockSpec(f4.shape, lambda b, j: (0, 0)),
                  rspec, rspec,
                  pl.BlockSpec((1, ch), lambda b, j: (0, 0))],
        out_specs=rspec,
        out_shape=jax.ShapeDtypeStruct((bsz, n // 2, nb, DFT_ROWS, ch), BF16),
        compiler_params=_cp("parallel", "parallel"),
        name="dft_stage4",
    )(c, f4, nat(x0c), nat(gv), f_bias.astype(F32).reshape(1, ch))
    return out.reshape(bsz, seq, ch)


def _outproj_kernel(om_ref, os_ref, oh_ref, x_ref, mod_ref, w_ref, g_ref, b_ref, o_ref):
    n_m, n_s = om_ref.shape[2], os_ref.shape[2]
    mixed = jnp.dot(om_ref[0], w_ref[0, 0:n_m], preferred_element_type=F32)
    mixed += jnp.dot(os_ref[0], w_ref[0, n_m:n_m + n_s], preferred_element_type=F32)
    mixed += jnp.dot(oh_ref[0], w_ref[0, n_m + n_s:], preferred_element_type=F32)
    gate = mod_ref[0][2:3]
    o_ref[0] = _layer_norm(ALPHA * x_ref[0] + gate * mixed, g_ref[...], b_ref[...])


def _outproj(o_mla, o_s5, o_hy, x, mod, w, l, ln_g, ln_b):
    bsz, seq, d = x.shape
    tm = min(512, seq)
    row = lambda width: pl.BlockSpec((1, tm, width), lambda b, i: (b, i, 0))
    const = lambda shp: pl.BlockSpec(shp, lambda b, i: (0,) * len(shp))
    return pl.pallas_call(
        _outproj_kernel,
        grid=(bsz, seq // tm),
        in_specs=[row(o_mla.shape[2]), row(o_s5.shape[2]), row(o_hy.shape[2]), row(d),
                  pl.BlockSpec((1, 6, d), lambda b, i: (b, 0, 0)),
                  pl.BlockSpec((1,) + w.shape[1:], lambda b, i: (l, 0, 0)), const((1, d)), const((1, d))],
        out_specs=row(d),
        out_shape=jax.ShapeDtypeStruct((bsz, seq, d), F32),
        compiler_params=_cp("parallel", "parallel"),
        name="out_proj_ln",
    )(o_mla, o_s5, o_hy, x, mod, w, ln_g.reshape(1, -1), ln_b.reshape(1, -1))


def _ffn_kernel(xp_ref, x_ref, xn_ref, mod_ref, wg_ref, wu_ref, cw_ref, cb_ref, wd_ref,
                g_ref, b_ref, o_ref, u_scr, acc_scr):
    i, j = pl.program_id(1), pl.program_id(2)
    tm = x_ref.shape[1]
    halo = xp_ref.shape[1]

    @pl.when(j == 0)
    def _():
        m = mod_ref[0]
        sc, sh = 1.0 + m[4:5], m[3:4]
        u_scr[0:halo] = (xp_ref[0] * sc + sh).astype(BF16)
        u_scr[halo:halo + tm] = (x_ref[0] * sc + sh).astype(BF16)
        u_scr[halo + tm:] = (xn_ref[0] * sc + sh).astype(BF16)
        acc_scr[...] = jnp.zeros_like(acc_scr)

    u = u_scr[...]
    rows = tm + 2 * halo
    gx = jnp.dot(u, wg_ref[0], preferred_element_type=F32)
    r = lax.broadcasted_iota(jnp.int32, (tm, 1), 0)
    keep_prev = jnp.logical_or(r > 0, i > 0)
    keep_next = jnp.logical_or(r < tm - 1, i < pl.num_programs(1) - 1)
    g_prev = jnp.where(keep_prev, pltpu.roll(gx, 1, 0)[halo:halo + tm], 0.0)
    g_next = jnp.where(keep_next, pltpu.roll(gx, rows - 1, 0)[halo:halo + tm], 0.0)
    cw = cw_ref[...]
    conv = g_prev * cw[0:1] + gx[halo:halo + tm] * cw[1:2] + g_next * cw[2:3] + cb_ref[...]
    up = jnp.dot(u[halo:halo + tm], wu_ref[0], preferred_element_type=F32)
    h = (conv * jax.nn.sigmoid(conv) * up).astype(BF16)
    acc_scr[...] += jnp.dot(h, wd_ref[0], preferred_element_type=F32)

    @pl.when(j == pl.num_programs(2) - 1)
    def _():
        gate = mod_ref[0][5:6]
        o_ref[0] = _layer_norm(ALPHA * x_ref[0] + gate * acc_scr[...], g_ref[...], b_ref[...])


def _ffn(x, mod, wg, wu, conv_w, conv_b, wd, l, ln_g, ln_b):
    bsz, seq, d = x.shape
    ff = wg.shape[2]
    tm = min(512, seq)
    tn = 512
    halo = SUBLANE
    nhb = seq // halo
    per = tm // halo
    const = lambda shp: pl.BlockSpec(shp, lambda b, i, j: (0,) * len(shp))
    return pl.pallas_call(
        _ffn_kernel,
        grid=(bsz, seq // tm, ff // tn),
        in_specs=[pl.BlockSpec((1, halo, d), lambda b, i, j: (b, jnp.maximum(i * per - 1, 0), 0)),
                  pl.BlockSpec((1, tm, d), lambda b, i, j: (b, i, 0)),
                  pl.BlockSpec((1, halo, d), lambda b, i, j: (b, jnp.minimum((i + 1) * per, nhb - 1), 0)),
                  pl.BlockSpec((1, 6, d), lambda b, i, j: (b, 0, 0)),
                  pl.BlockSpec((1, d, tn), lambda b, i, j: (l, 0, j)),
                  pl.BlockSpec((1, d, tn), lambda b, i, j: (l, 0, j)),
                  pl.BlockSpec((3, tn), lambda b, i, j: (0, j)),
                  pl.BlockSpec((1, tn), lambda b, i, j: (0, j)),
                  pl.BlockSpec((1, tn, d), lambda b, i, j: (l, j, 0)),
                  const((1, d)), const((1, d))],
        out_specs=pl.BlockSpec((1, tm, d), lambda b, i, j: (b, i, 0)),
        out_shape=jax.ShapeDtypeStruct((bsz, seq, d), F32),
        scratch_shapes=[pltpu.VMEM((tm + 2 * halo, d), BF16), pltpu.VMEM((tm, d), F32)],
        compiler_params=_cp("parallel", "parallel", "arbitrary"),
        name="conv_ffn_ln",
    )(x, x, x, mod, wg, wu, conv_w.astype(F32), conv_b.reshape(1, -1).astype(F32), wd,
      ln_g.reshape(1, -1), ln_b.reshape(1, -1))


def _rot_half_cols(w):
    half = w.shape[-1] // 2
    return jnp.concatenate([-w[..., half:], w[..., :half]], axis=-1)


def _prep_w_in(w):
    q, kv, kr, s5, hy = jnp.split(w, (512, 768, 832, 1344), axis=-1)
    return jnp.concatenate([q, hy, kv, kr, _rot_half_cols(kr), s5], axis=-1).astype(BF16)


def _prep_w_uq(w):
    w = w.reshape(w.shape[:-1] + (MLA_HEADS, MLA_QK))
    pe = w[..., MLA_NOPE:]
    return jnp.concatenate([w, _rot_half_cols(pe)], axis=-1).reshape(w.shape[:-2] + (-1,)).astype(BF16)


def kernel(x, c, positions, ada_w, ada_b, w_in, q_norm_g, kv_norm_g, w_uq, w_ukv, s5_a_re, s5_a_im, s5_log_step, s5_b_re, s5_b_im, s5_c_re, s5_c_im, s5_d, s5_glu_w, s5_glu_b, hy_conv_w, hy_conv_b, hy_f_w1, hy_f_b1, hy_f_freq, hy_f_w2, hy_f_b2, hy_f_w3, hy_f_bias, w_out, ln1_g, ln1_b, ffn_w_gate, ffn_w_up, ffn_conv_w, ffn_conv_b, ffn_w_down, ln2_g, ln2_b):
    bsz, seq, _ = x.shape
    depth = ada_w.shape[0]
    mods = _ada(c, ada_w, ada_b)
    tables = _dft_tables(math.isqrt(2 * seq))
    w_in_b = _prep_w_in(w_in)
    w_uq_b = _prep_w_uq(w_uq)
    w_ukv_b = w_ukv.astype(BF16)
    w_out_b = w_out.astype(BF16)
    wg_b, wu_b, wd_b = ffn_w_gate.astype(BF16), ffn_w_up.astype(BF16), ffn_w_down.astype(BF16)
    s5_all = jax.vmap(_s5_params)(s5_a_re, s5_a_im, s5_log_step, s5_b_re, s5_b_im, s5_c_re, s5_c_im, s5_d)
    for l in range(depth):
        mod = mods[l]
        proj, xs5 = _inproj(x, mod, w_in_b, l)
        q, k, v = _mla_prep(proj, positions, q_norm_g[l], kv_norm_g[l], w_uq_b[l], w_ukv_b[l])
        o_mla = _attention(q, k, v)
        o_s5 = _s5(xs5, s5_all, l, s5_glu_w[l], s5_glu_b[l])
        o_hy = _hyena(proj, tables, hy_conv_w[l], hy_conv_b[l], hy_f_w1[l], hy_f_b1[l],
                      hy_f_freq[l], hy_f_w2[l], hy_f_b2[l], hy_f_w3[l], hy_f_bias[l])
        x = _outproj(o_mla, o_s5, o_hy, x, mod, w_out_b, l, ln1_g[l], ln1_b[l])
        x = _ffn(x, mod, wg_b, wu_b, ffn_conv_w[l], ffn_conv_b[l], wd_b, l, ln2_g[l], ln2_b[l])
    return x
```

```python
import functools
import math

import numpy as np
import jax
import jax.numpy as jnp
from jax import lax
from jax.experimental import pallas as pl
from jax.experimental.pallas import tpu as pltpu

F32 = jnp.float32
BF16 = jnp.bfloat16

MLA_HEADS = 8
MLA_NOPE = 128
MLA_ROPE = 64
MLA_V = 128
MLA_QK = MLA_NOPE + MLA_ROPE
MLA_Q_RANK = 512
MLA_KV_RANK = 256
ROPE_THETA = 10000.0
S5_W = 512
S5_GROUP = 16
S5_GROUPS = 32
S5_STATE = 64
S5_CHUNK = 16
S5_OCT = 8
HY_W = 512
HY_BANDS = 16
HY_DECAY_TARGET = 1e-2
HY_FAST_PCT = 0.3
HY_SLOW_PCT = 1.5
LN_EPS = 1e-5
RMS_EPS = 1e-6
FILTER_EPS = 1e-6
DEPTH = 2
ALPHA = (2 * DEPTH) ** 0.25

COL_Q = 0
COL_HY = 512
COL_KV = 2048
COL_KR = 2304
COL_S5 = 2432
IN_COLS_PAD = 2944

LANE = 128
SUBLANE = 8
VMEM_LIMIT = 56 * 1024 * 1024


def _cp(*sem):
    return pltpu.CompilerParams(dimension_semantics=sem, vmem_limit_bytes=VMEM_LIMIT)


def _layer_norm(y, g, b):
    mu = jnp.mean(y, axis=-1, keepdims=True)
    d = y - mu
    var = jnp.mean(d * d, axis=-1, keepdims=True)
    return d * lax.rsqrt(var + LN_EPS) * g + b


def _ada_kernel(c_ref, w_ref, b_ref, o_ref):
    c = c_ref[...]
    cond = c * jax.nn.sigmoid(c)
    o_ref[0] = jnp.dot(cond, w_ref[0], preferred_element_type=F32) + b_ref[0]


def _ada(c, ada_w, ada_b):
    bsz, d = c.shape
    depth, _, n = ada_w.shape
    tn = 1024
    cp = jnp.zeros((SUBLANE, d), F32).at[:bsz].set(c)
    out = pl.pallas_call(
        _ada_kernel,
        grid=(depth, n // tn),
        in_specs=[pl.BlockSpec((SUBLANE, d), lambda l, j: (0, 0)),
                  pl.BlockSpec((1, d, tn), lambda l, j: (l, 0, j)),
                  pl.BlockSpec((1, 1, tn), lambda l, j: (l, 0, j))],
        out_specs=pl.BlockSpec((1, SUBLANE, tn), lambda l, j: (l, 0, j)),
        out_shape=jax.ShapeDtypeStruct((depth, SUBLANE, n), F32),
        compiler_params=_cp("parallel", "parallel"),
        name="ada_mod",
    )(cp, ada_w, ada_b.reshape(depth, 1, n))
    return out[:, :bsz].reshape(depth, bsz, 6, d)


def _chunk_perm(chunks, to_time_major):
    q = S5_CHUNK
    nat = np.arange(chunks * q).reshape(chunks, q)
    tmaj = nat.T.reshape(-1)
    p = np.zeros((chunks * q, chunks * q), np.float32)
    if to_time_major:
        p[np.arange(chunks * q), tmaj] = 1.0
    else:
        p[tmaj, np.arange(chunks * q)] = 1.0
    return jnp.asarray(p, BF16)


def _inproj_kernel(x_ref, mod_ref, w_ref, perm_ref, o_ref, xs_ref):
    m = mod_ref[0]
    u = x_ref[0] * (1.0 + m[1:2]) + m[0:1]
    res = jnp.dot(u.astype(BF16), w_ref[0], preferred_element_type=F32)
    o_ref[0] = res[:, :COL_S5]
    us = jnp.dot(perm_ref[...], res[:, COL_S5:].astype(BF16), preferred_element_type=F32).astype(BF16)
    cn = xs_ref.shape[1]
    oct_w = S5_CHUNK * LANE
    for t in range(S5_CHUNK):
        for o in range(S5_W // LANE):
            xs_ref[0, :, o * oct_w + t * LANE:o * oct_w + (t + 1) * LANE] = (
                us[t * cn:(t + 1) * cn, o * LANE:(o + 1) * LANE])


def _inproj(x, mod, w, l):
    bsz, seq, d = x.shape
    n = w.shape[2]
    tm = min(256, seq)
    cn = tm // S5_CHUNK
    return pl.pallas_call(
        _inproj_kernel,
        grid=(bsz, seq // tm),
        in_specs=[pl.BlockSpec((1, tm, d), lambda b, i: (b, i, 0)),
                  pl.BlockSpec((1, 6, d), lambda b, i: (b, 0, 0)),
                  pl.BlockSpec((1, d, n), lambda b, i: (l, 0, 0)),
                  pl.BlockSpec((tm, tm), lambda b, i: (0, 0))],
        out_specs=[pl.BlockSpec((1, tm, COL_S5), lambda b, i: (b, i, 0)),
                   pl.BlockSpec((1, cn, S5_CHUNK * S5_W), lambda b, i: (b, i, 0))],
        out_shape=[jax.ShapeDtypeStruct((bsz, seq, COL_S5), F32),
                   jax.ShapeDtypeStruct((bsz, seq // S5_CHUNK, S5_CHUNK * S5_W), BF16)],
        compiler_params=_cp("parallel", "parallel"),
        name="in_proj",
    )(x, mod, w, _chunk_perm(cn, True))


def _rope_trig(ang):
    fw = MLA_ROPE // 2
    nb = LANE // fw
    rq = ang.shape[0] // nb
    blk = lax.broadcasted_iota(jnp.int32, (rq, LANE), 1) // fw
    packed = ang[0:rq]
    for b in range(1, nb):
        packed = jnp.where(blk == b, ang[b * rq:(b + 1) * rq], packed)
    outs = []
    for t in (jnp.cos(packed), jnp.sin(packed)):
        rows = []
        for b in range(nb):
            one = jnp.where(blk == b, t, 0.0)
            full = one
            for r in range(1, nb):
                full = full + pltpu.roll(one, r * fw, 1)
            rows.append(full)
        outs.append(jnp.concatenate(rows, axis=0))
    return outs


def _mla_prep_kernel(ql_ref, kvl_ref, kr_ref, pos_ref, invf_ref, qg_ref, kvg_ref,
                     wq_ref, wkv_ref, q_ref, k_ref, v_ref):
    scale = MLA_QK ** -0.5 * math.log2(math.e)
    ql = ql_ref[0]
    qn = ql * lax.rsqrt(jnp.mean(ql * ql, axis=-1, keepdims=True) + RMS_EPS) * qg_ref[...]
    kvl = kvl_ref[0]
    kvn = kvl * lax.rsqrt(jnp.mean(kvl * kvl, axis=-1, keepdims=True) + RMS_EPS) * kvg_ref[...]
    ang = pos_ref[0] * invf_ref[...]
    cos, sin = _rope_trig(ang)
    qa = jnp.dot(qn.astype(BF16), wq_ref[...], preferred_element_type=F32)
    kva = jnp.dot(kvn.astype(BF16), wkv_ref[...], preferred_element_type=F32)
    kr = kr_ref[0]
    kpe = (kr * cos + pltpu.roll(kr, MLA_ROPE, 1) * sin)[:, :MLA_ROPE].astype(BF16)
    for h in range(MLA_HEADS):
        c0 = 2 * LANE * h
        pr = qa[:, c0 + LANE:c0 + 2 * LANE]
        qpe = pr * cos + pltpu.roll(pr, MLA_ROPE, 1) * sin
        q_ref[0, h, :, 0:MLA_NOPE] = (qa[:, c0:c0 + LANE] * scale).astype(BF16)
        q_ref[0, h, :, MLA_NOPE:MLA_QK] = (qpe[:, :MLA_ROPE] * scale).astype(BF16)
        k_ref[0, h, :, 0:MLA_NOPE] = kva[:, c0:c0 + LANE].astype(BF16)
        k_ref[0, h, :, MLA_NOPE:MLA_QK] = kpe
        v_ref[0, h] = kva[:, c0 + LANE:c0 + 2 * LANE].astype(BF16)


def _mla_prep(proj, positions, q_g, kv_g, wq, wkv):
    bsz, seq, _ = proj.shape
    tm = min(512, seq)
    inv_freq = ROPE_THETA ** (-jnp.arange(0, MLA_ROPE, 2, dtype=F32) / MLA_ROPE)
    invf = jnp.tile(inv_freq, LANE // (MLA_ROPE // 2)).reshape(1, LANE)
    pos = positions.astype(F32).reshape(bsz, seq, 1)
    hshape = lambda w: jax.ShapeDtypeStruct((bsz, MLA_HEADS, seq, w), BF16)
    hspec = lambda w: pl.BlockSpec((1, MLA_HEADS, tm, w), lambda b, i: (b, 0, i, 0))
    const = lambda shp: pl.BlockSpec(shp, lambda b, i: (0,) * len(shp))
    return pl.pallas_call(
        _mla_prep_kernel,
        grid=(bsz, seq // tm),
        in_specs=[pl.BlockSpec((1, tm, MLA_Q_RANK), lambda b, i: (b, i, COL_Q // MLA_Q_RANK)),
                  pl.BlockSpec((1, tm, MLA_KV_RANK), lambda b, i: (b, i, COL_KV // MLA_KV_RANK)),
                  pl.BlockSpec((1, tm, LANE), lambda b, i: (b, i, COL_KR // LANE)),
                  pl.BlockSpec((1, tm, 1), lambda b, i: (b, i, 0)),
                  const((1, LANE)), const((1, MLA_Q_RANK)), const((1, MLA_KV_RANK)),
                  const(wq.shape), const(wkv.shape)],
        out_specs=[hspec(MLA_QK), hspec(MLA_QK), hspec(MLA_V)],
        out_shape=[hshape(MLA_QK), hshape(MLA_QK), hshape(MLA_V)],
        compiler_params=_cp("parallel", "parallel"),
        name="mla_prep",
    )(proj, proj, proj, pos, invf, q_g.reshape(1, -1), kv_g.reshape(1, -1), wq, wkv)


def _attn_kernel(q_ref, k_ref, v_ref, o_ref, *, tk):
    q = q_ref[0, 0]
    tq = q.shape[0]
    nk = k_ref.shape[2] // tk

    def body(j, carry):
        m, l, acc = carry
        off = pl.multiple_of(j * tk, tk)
        kj = k_ref[0, 0, pl.ds(off, tk), :]
        vj = v_ref[0, 0, pl.ds(off, tk), :]
        s = lax.dot_general(q, kj, (((1,), (1,)), ((), ())), preferred_element_type=F32)
        m_new = jnp.maximum(m, jnp.max(s, axis=-1, keepdims=True))
        a = jnp.exp2(m - m_new)
        p = jnp.exp2(s - m_new)
        l = a * l + jnp.sum(p, axis=-1, keepdims=True)
        acc = a * acc + jnp.dot(p.astype(BF16), vj, preferred_element_type=F32)
        return m_new, l, acc

    init = (jnp.full((tq, 1), jnp.finfo(F32).min, F32), jnp.zeros((tq, 1), F32),
            jnp.zeros((tq, MLA_V), F32))
    _, l, acc = lax.fori_loop(0, nk, body, init, unroll=True)
    o_ref[0] = (acc / l).astype(o_ref.dtype)


def _attention(q, k, v):
    bsz, nh, seq, _ = q.shape
    tq = min(1024, seq)
    tk = min(2048, seq)
    return pl.pallas_call(
        functools.partial(_attn_kernel, tk=tk),
        grid=(bsz, nh, seq // tq),
        in_specs=[pl.BlockSpec((1, 1, tq, MLA_QK), lambda b, h, i: (b, h, i, 0)),
                  pl.BlockSpec((1, 1, seq, MLA_QK), lambda b, h, i: (b, h, 0, 0)),
                  pl.BlockSpec((1, 1, seq, MLA_V), lambda b, h, i: (b, h, 0, 0))],
        out_specs=pl.BlockSpec((1, tq, MLA_V), lambda b, h, i: (b, i, h)),
        out_shape=jax.ShapeDtypeStruct((bsz, seq, nh * MLA_V), BF16),
        compiler_params=_cp("parallel", "parallel", "parallel"),
        name="mla_attention",
    )(q, k, v)


def _s5_params(a_re, a_im, log_step, b_re, b_im, c_re, c_im, d):
    q = S5_CHUNK
    g, p, hh = S5_GROUPS, S5_STATE, S5_GROUP
    j = S5_OCT
    no = g // j
    ks = jnp.arange(q + 1, dtype=F32)[:, None, None]
    out = {}
    state_tabs, out_tabs, c0_tabs = [], [], []

    def oct_rows(w):
        t = w.shape[1]
        return w.reshape(no, j, t, hh, p).transpose(0, 2, 1, 3, 4).reshape(no, t * j * hh, p)

    for direction in range(2):
        ar, ai = a_re[direction].astype(F32), a_im[direction].astype(F32)
        step = jnp.exp(log_step[direction].astype(F32))[:, None]
        mag = jnp.exp(step * ar)
        abar_r = mag * jnp.cos(step * ai)
        abar_i = mag * jnp.sin(step * ai)
        den = ar * ar + ai * ai
        nr = abar_r - 1.0
        fr = (nr * ar + abar_i * ai) / den
        fi = (abar_i * ar - nr * ai) / den
        pmag = jnp.exp(ks * (step * ar)[None])
        pw_r = pmag * jnp.cos(ks * (step * ai)[None])
        pw_i = pmag * jnp.sin(ks * (step * ai)[None])
        br, bi = b_re[direction].astype(F32), b_im[direction].astype(F32)
        bf_r = fr[..., None] * br - fi[..., None] * bi
        bf_i = fr[..., None] * bi + fi[..., None] * br
        cr, ci = c_re[direction].astype(F32), c_im[direction].astype(F32)
        bt_r, bt_i = bf_r.transpose(0, 2, 1)[:, None], bf_i.transpose(0, 2, 1)[:, None]
        tpow = (q - 1 - jnp.arange(q)) if direction == 0 else jnp.arange(q)
        sr = pw_r[tpow].transpose(1, 0, 2)[:, :, None, :]
        si = pw_i[tpow].transpose(1, 0, 2)[:, :, None, :]
        state_tabs += [oct_rows(sr * bt_r - si * bt_i), oct_rows(sr * bt_i + si * bt_r)]
        opow = (jnp.arange(q) + 1) if direction == 0 else (q - jnp.arange(q))
        orr = pw_r[opow].transpose(1, 0, 2)[:, :, None, :]
        oi = pw_i[opow].transpose(1, 0, 2)[:, :, None, :]
        out_tabs += [oct_rows(cr[:, None] * orr - ci[:, None] * oi),
                     oct_rows(-(cr[:, None] * oi + ci[:, None] * orr))]
        c0_tabs += [oct_rows(cr[:, None]), oct_rows(-ci[:, None])]
        out[f'a_r{direction}'] = pw_r[q].reshape(1, g * p)
        out[f'a_i{direction}'] = pw_i[q].reshape(1, g * p)
    out['tab'] = jnp.stack(state_tabs + out_tabs).astype(BF16)
    out['c0'] = jnp.stack(c0_tabs).astype(BF16)
    out['dvec'] = d.astype(F32).reshape(no, 1, j * hh)
    return out


def _expand_const():
    p, j = S5_STATE, S5_OCT
    return jnp.asarray(np.tile(np.eye(p, dtype=np.float32), (1, j)), BF16)


def _expand(block, e_ref, same):
    w = jnp.dot(block, e_ref[...], preferred_element_type=F32)
    return jnp.where(same, w, 0.0).astype(BF16)


def _same_group_mask():
    hh, p, j = S5_GROUP, S5_STATE, S5_OCT
    shape = (j * hh, j * p)
    return (lax.broadcasted_iota(jnp.int32, shape, 0) // hh) == (lax.broadcasted_iota(jnp.int32, shape, 1) // p)


def _s5_state_kernel(x_ref, tab_ref, e_ref, vfr_ref, vfi_ref, vbr_ref, vbi_ref, w_scr):
    q, bw, sw = S5_CHUNK, S5_OCT * S5_GROUP, S5_OCT * S5_STATE

    @pl.when(pl.program_id(1) == 0)
    def _():
        same = _same_group_mask()
        for t in range(q):
            rows = slice(t * bw, (t + 1) * bw)
            for part in range(4):
                w_scr[rows, part * sw:(part + 1) * sw] = _expand(tab_ref[0, part, 0, rows, :], e_ref, same)

    v = jnp.dot(x_ref[...], w_scr[...], preferred_element_type=F32)
    w = vfr_ref.shape[1]
    vfr_ref[...] = v[:, 0 * w:1 * w]
    vfi_ref[...] = v[:, 1 * w:2 * w]
    vbr_ref[...] = v[:, 2 * w:3 * w]
    vbi_ref[...] = v[:, 3 * w:4 * w]


def _s5_scan_kernel(vfr_ref, vfi_ref, vbr_ref, vbi_ref, afr_ref, afi_ref, abr_ref, abi_ref,
                    sfr_ref, sfi_ref, sbr_ref, sbi_ref):
    nb, nc = vfr_ref.shape[0], vfr_ref.shape[1]
    afr, afi, abr, abi = afr_ref[0], afi_ref[0], abr_ref[0], abi_ref[0]

    def body(c, carry):
        cb = nc - 1 - c
        new = []
        for b in range(nb):
            fr, fi, br, bi = carry[4 * b:4 * b + 4]
            sfr_ref[b, pl.ds(c, 1), :] = fr
            sfi_ref[b, pl.ds(c, 1), :] = fi
            sbr_ref[b, pl.ds(cb, 1), :] = br
            sbi_ref[b, pl.ds(cb, 1), :] = bi
            new += [afr * fr - afi * fi + vfr_ref[b, pl.ds(c, 1), :],
                    afr * fi + afi * fr + vfi_ref[b, pl.ds(c, 1), :],
                    abr * br - abi * bi + vbr_ref[b, pl.ds(cb, 1), :],
                    abr * bi + abi * br + vbi_ref[b, pl.ds(cb, 1), :]]
        return tuple(new)

    z = jnp.zeros((1, vfr_ref.shape[2]), F32)
    lax.fori_loop(0, nc, body, (z,) * (4 * nb))


def _dot_nt(a, b):
    return lax.dot_general(a, b, (((1,), (1,)), ((), ())), preferred_element_type=F32)


def _s5_out_kernel(x_ref, tabs_ref, tabo_ref, c0_ref, d_ref, sfr_ref, sfi_ref, sbr_ref, sbi_ref, e_ref,
                   y_ref, m_scr, cp_scr):
    q, bw = S5_CHUNK, S5_OCT * S5_GROUP

    @pl.when(pl.program_id(1) == 0)
    def _():
        same = _same_group_mask()
        c0 = [_expand(c0_ref[0, part, 0], e_ref, same) for part in range(4)]
        lag_f, lag_b = [None] * q, [None] * q
        for t in range(q):
            rows = slice(t * bw, (t + 1) * bw)
            for part in range(4):
                cp_scr[part, rows, :] = _expand(tabo_ref[0, part, 0, rows, :], e_ref, same)
            ws = [_expand(tabs_ref[0, part, 0, rows, :], e_ref, same) for part in range(4)]
            lag_f[q - 1 - t] = _dot_nt(ws[0], c0[0]) + _dot_nt(ws[1], c0[1])
            lag_b[t] = _dot_nt(ws[2], c0[2]) + _dot_nt(ws[3], c0[3])
        eye = (lax.broadcasted_iota(jnp.int32, (bw, bw), 0) == lax.broadcasted_iota(jnp.int32, (bw, bw), 1))
        lag0 = (lag_f[0] + lag_b[0] + jnp.where(eye, d_ref[0, 0], 0.0)).astype(BF16)
        lag_f = [v.astype(BF16) for v in lag_f]
        lag_b = [v.astype(BF16) for v in lag_b]
        for ti in range(q):
            for to in range(q):
                blk = lag0 if to == ti else (lag_f[to - ti] if to > ti else lag_b[ti - to])
                m_scr[ti * bw:(ti + 1) * bw, to * bw:(to + 1) * bw] = blk

    y = jnp.dot(x_ref[...], m_scr[...], preferred_element_type=F32)
    for part, s_ref in enumerate((sfr_ref, sfi_ref, sbr_ref, sbi_ref)):
        y += _dot_nt(s_ref[...].astype(BF16), cp_scr[part])
    y_ref[...] = y


def _s5_post_kernel(y_ref, w_ref, b_ref, perm_ref, o_ref):
    oct_w = S5_CHUNK * LANE
    y = jnp.concatenate(
        [jnp.concatenate([y_ref[:, o * oct_w + t * LANE:o * oct_w + (t + 1) * LANE]
                          for o in range(S5_W // LANE)], axis=1)
         for t in range(S5_CHUNK)], axis=0)
    y = jax.nn.gelu(y)
    z = jnp.dot(y.astype(BF16), w_ref[...], preferred_element_type=F32) + b_ref[...]
    o = (y * jax.nn.sigmoid(z)).astype(BF16)
    o_ref[...] = jnp.dot(perm_ref[...], o, preferred_element_type=F32).astype(o_ref.dtype)


def _s5(xs, prm, l, glu_w, glu_b):
    bsz, nc, _ = xs.shape
    q, g, p = S5_CHUNK, S5_GROUPS, S5_STATE
    seq = nc * q
    no = g // S5_OCT
    rows = bsz * nc
    feat = q * LANE
    sw = S5_OCT * p
    nst = g * p
    x = xs.reshape(rows, no * feat)
    e = _expand_const()
    tab = prm['tab']
    tab_spec = lambda half: pl.BlockSpec((1, 4, 1, feat, p), lambda o, i: (l, half, o, 0, 0))
    espec = pl.BlockSpec(e.shape, lambda o, i: (0, 0))
    rt = min(512, rows)
    xspec = pl.BlockSpec((rt, feat), lambda o, i: (i, o))
    vspec = pl.BlockSpec((rt, sw), lambda o, i: (i, o))
    vshape = jax.ShapeDtypeStruct((rows, nst), F32)
    v4 = pl.pallas_call(
        _s5_state_kernel,
        grid=(no, rows // rt),
        in_specs=[xspec, tab_spec(0), espec],
        out_specs=[vspec] * 4,
        out_shape=[vshape] * 4,
        scratch_shapes=[pltpu.VMEM((feat, 4 * sw), BF16)],
        compiler_params=_cp("parallel", "arbitrary"),
        name="s5_state",
    )(x, tab, e)

    lt = min(512, nst)
    v4 = [v.reshape(bsz, nc, nst) for v in v4]
    sspec = pl.BlockSpec((bsz, nc, lt), lambda n: (0, 0, n))
    aspec = pl.BlockSpec((1, 1, lt), lambda n: (l, 0, n))
    s4 = pl.pallas_call(
        _s5_scan_kernel,
        grid=(nst // lt,),
        in_specs=[sspec] * 4 + [aspec] * 4,
        out_specs=[sspec] * 4,
        out_shape=[jax.ShapeDtypeStruct((bsz, nc, nst), F32)] * 4,
        compiler_params=_cp("parallel"),
        name="s5_scan",
    )(*v4, prm['a_r0'], prm['a_i0'], prm['a_r1'], prm['a_i1'])
    s4 = [s.reshape(rows, nst) for s in s4]

    ro = min(256, rows)
    xspec = pl.BlockSpec((ro, feat), lambda o, i: (i, o))
    vspec = pl.BlockSpec((ro, sw), lambda o, i: (i, o))
    bw = S5_OCT * S5_GROUP
    y = pl.pallas_call(
        _s5_out_kernel,
        grid=(no, rows // ro),
        in_specs=([xspec, tab_spec(0), tab_spec(1),
                   pl.BlockSpec((1, 4, 1, bw, p), lambda o, i: (l, 0, o, 0, 0)),
                   pl.BlockSpec((1, 1, 1, bw), lambda o, i: (l, o, 0, 0))] + [vspec] * 4 + [espec]),
        out_specs=xspec,
        out_shape=jax.ShapeDtypeStruct((rows, no * feat), F32),
        scratch_shapes=[pltpu.VMEM((feat, feat), BF16), pltpu.VMEM((4, feat, sw), BF16)],
        compiler_params=_cp("parallel", "arbitrary"),
        name="s5_out",
    )(x, tab, tab, prm['c0'], prm['dvec'], *s4, e)

    cn = min(32, rows)
    const = lambda shp: pl.BlockSpec(shp, lambda i: (0,) * len(shp))
    out = pl.pallas_call(
        _s5_post_kernel,
        grid=(rows // cn,),
        in_specs=[pl.BlockSpec((cn, no * feat), lambda i: (i, 0)),
                  const((S5_W, S5_W)), const((1, S5_W)), const((cn * q, cn * q))],
        out_specs=pl.BlockSpec((cn * q, S5_W), lambda i: (i, 0)),
        out_shape=jax.ShapeDtypeStruct((rows * q, S5_W), BF16),
        compiler_params=_cp("parallel"),
        name="s5_post",
    )(y, glu_w.astype(BF16), glu_b.reshape(1, -1).astype(F32), _chunk_perm(cn, False))
    return out.reshape(bsz, seq, S5_W)


def _hy_prep_kernel(x0_ref, x1_ref, v_ref, w_ref, b_ref, x0c_ref, gv_ref):
    seq = x0_ref.shape[1]
    row = lax.broadcasted_iota(jnp.int32, (seq, LANE), 0)
    first = row == 0
    last = row == seq - 1

    def conv(ref, k):
        u = ref[0]
        w = w_ref[k]
        prev = jnp.where(first, 0.0, pltpu.roll(u, 1, 0))
        nxt = jnp.where(last, 0.0, pltpu.roll(u, seq - 1, 0))
        return prev * w[0:1] + u * w[1:2] + nxt * w[2:3] + b_ref[k]

    x0c_ref[0] = conv(x0_ref, 0)
    gv_ref[0] = conv(x1_ref, 1) * conv(v_ref, 2)


def _hy_prep(proj, conv_w, conv_b):
    bsz, seq, _ = proj.shape
    nt = HY_W // LANE
    base = COL_HY // LANE
    w = conv_w.astype(F32).reshape(3, 3, HY_W).transpose(1, 0, 2)
    b = conv_b.astype(F32).reshape(3, 1, HY_W)
    col = lambda part: pl.BlockSpec((1, seq, LANE), lambda bb, c: (bb, 0, base + part * nt + c))
    ospec = pl.BlockSpec((1, seq, LANE), lambda bb, c: (bb, 0, c))
    return pl.pallas_call(
        _hy_prep_kernel,
        grid=(bsz, nt),
        in_specs=[col(0), col(1), col(2),
                  pl.BlockSpec((3, 3, LANE), lambda bb, c: (0, 0, c)),
                  pl.BlockSpec((3, 1, LANE), lambda bb, c: (0, 0, c))],
        out_specs=[ospec, ospec],
        out_shape=[jax.ShapeDtypeStruct((bsz, seq, HY_W), F32)] * 2,
        compiler_params=_cp("parallel", "parallel"),
        name="hyena_prep",
    )(proj, proj, proj, w, b)


def _hy_filter_kernel(feat_ref, w1_ref, b1_ref, fq_ref, w2_ref, b2_ref, w3f_ref, w3b_ref,
                      dl_ref, o_ref, h_scr):
    @pl.when(pl.program_id(0) == 0)
    def _():
        fq = fq_ref[...]
        h1 = jnp.sin(fq[0:1] * (jnp.dot(feat_ref[...], w1_ref[...], preferred_element_type=F32) + b1_ref[...]))
        h_scr[...] = jnp.sin(fq[1:2] * (jnp.dot(h1, w2_ref[...], preferred_element_type=F32) + b2_ref[...]))

    h = h_scr[...]
    win = jnp.exp(-feat_ref[:, 0:1] * dl_ref[...])
    hq = h.astype(BF16)
    hf = jnp.dot(hq, w3f_ref[...].astype(BF16), preferred_element_type=F32) * win
    hb = jnp.dot(hq, w3b_ref[...].astype(BF16), preferred_element_type=F32) * win
    row = lax.broadcasted_iota(jnp.int32, hb.shape, 0)
    hb = jnp.where(row == 0, 0.0, hb)
    ss = jnp.sum(hf * hf, axis=0, keepdims=True) + jnp.sum(hb * hb, axis=0, keepdims=True)
    scale = lax.rsqrt(ss + FILTER_EPS)
    o_ref[0] = hf * scale
    o_ref[1] = hb * scale


def _hy_filter(seq, w1, b1, freq, w2, b2, w3):
    t01 = jnp.linspace(0.0, 1.0, seq, dtype=F32)[:, None]
    w = 2.0 * math.pi * jnp.arange(seq, dtype=F32)[:, None] / seq
    bands = jnp.linspace(1e-4, HY_BANDS - 1, HY_BANDS, dtype=F32)[None, :]
    fw = w * bands
    emb = 1 + 2 * HY_BANDS
    kpad = 64
    feat = jnp.concatenate([t01, jnp.cos(fw), -jnp.sin(fw), jnp.zeros((seq, kpad - emb), F32)], axis=-1)
    w1p = jnp.zeros((kpad, w1.shape[1]), F32).at[:emb].set(w1.astype(F32))
    deltas = jnp.abs(jnp.linspace(math.log(HY_DECAY_TARGET) / HY_SLOW_PCT,
                                  math.log(HY_DECAY_TARGET) / HY_FAST_PCT, HY_W, dtype=F32))[None, :]
    nt = HY_W // LANE
    hid = w2.shape[0]
    const = lambda shp: pl.BlockSpec(shp, lambda c: (0,) * len(shp))
    return pl.pallas_call(
        _hy_filter_kernel,
        grid=(nt,),
        in_specs=[const((seq, kpad)), const((kpad, hid)), const((1, hid)), const((2, hid)),
                  const((hid, hid)), const((1, hid)),
                  pl.BlockSpec((hid, LANE), lambda c: (0, c)),
                  pl.BlockSpec((hid, LANE), lambda c: (0, nt + c)),
                  pl.BlockSpec((1, LANE), lambda c: (0, c))],
        out_specs=pl.BlockSpec((2, seq, LANE), lambda c: (0, 0, c)),
        out_shape=jax.ShapeDtypeStruct((2, seq, HY_W), F32),
        scratch_shapes=[pltpu.VMEM((seq, hid), F32)],
        compiler_params=_cp("arbitrary"),
        name="hyena_filter",
    )(feat, w1p, b1.reshape(1, -1).astype(F32), freq.astype(F32), w2.astype(F32),
      b2.reshape(1, -1).astype(F32), w3.astype(F32), w3.astype(F32), deltas)


def _dft_tables(n):
    big = n * n
    kk = np.arange(n)
    ph1 = (2.0 * np.pi / n) * ((kk[:, None] * kk[None, :n // 2]) % n)
    f1 = np.stack([np.cos(ph1), -np.sin(ph1)], axis=1).reshape(2 * n, n // 2)
    eye = np.eye(SUBLANE)
    f4 = jnp.asarray(np.kron(f1.T / big, eye).astype(np.float32), BF16)
    f1 = jnp.asarray(np.kron(f1, eye).astype(np.float32), BF16)
    idx = (kk[None, None, :] * (kk[:, None, None] + n * kk[None, :, None])) % big
    th = (2.0 * np.pi / big) * idx
    gr, gi = np.cos(th), -np.sin(th)
    gfwd = np.concatenate([np.concatenate([gr, -gi], axis=2),
                           np.concatenate([gi, gr], axis=2)], axis=1)
    return f1, f4, jnp.asarray(gfwd.astype(np.float32), BF16)


DFT_ROWS = 2 * SUBLANE


def _kron_dot(f_ref, x3):
    k, _, ch = x3.shape
    halves = []
    for h in range(DFT_ROWS // SUBLANE):
        xh = x3[:, h * SUBLANE:(h + 1) * SUBLANE, :].reshape(k * SUBLANE, ch).astype(BF16)
        yh = jnp.dot(f_ref[...], xh, preferred_element_type=F32)
        halves.append(yh.reshape(-1, SUBLANE, ch))
    return jnp.concatenate(halves, axis=1)


def _dft1_kernel(x_ref, f_ref, o_ref):
    o_ref[0, :, 0] = _kron_dot(f_ref, x_ref[0, :, 0]).astype(o_ref.dtype)


def _dft1(x, f1, n):
    bsz, seq, ch = x.shape
    nb = n // DFT_ROWS
    return pl.pallas_call(
        _dft1_kernel,
        grid=(bsz, nb),
        in_specs=[pl.BlockSpec((1, n // 2, 1, DFT_ROWS, ch), lambda b, j: (b, 0, j, 0, 0)),
                  pl.BlockSpec(f1.shape, lambda b, j: (0, 0))],
        out_specs=pl.BlockSpec((1, 2 * n, 1, DFT_ROWS, ch), lambda b, j: (b, 0, j, 0, 0)),
        out_shape=jax.ShapeDtypeStruct((bsz, 2 * n, nb, DFT_ROWS, ch), BF16),
        compiler_params=_cp("parallel", "parallel"),
        name="dft_stage1",
    )(x.reshape(bsz, n // 2, nb, DFT_ROWS, ch), f1)


def _filt_spec_kernel(af_ref, ab_ref, g_ref, hre_ref, him_ref):
    n = g_ref.shape[1] // 2
    for i in range(g_ref.shape[0]):
        xf = jnp.dot(g_ref[i], af_ref[0, i], preferred_element_type=F32)
        xb = jnp.dot(g_ref[i], ab_ref[0, i], preferred_element_type=F32)
        hre_ref[i] = xf[:n] + xb[:n]
        him_ref[i] = xf[n:] - xb[n:]


def _conv_spec_kernel(a_ref, gf_ref, hre_ref, him_ref, c_ref):
    n = gf_ref.shape[1] // 2
    for i in range(gf_ref.shape[0]):
        x = jnp.dot(gf_ref[i], a_ref[0, i], preferred_element_type=F32)
        xr, xi = x[:n], x[n:]
        hr, hi = hre_ref[i], him_ref[i]
        pr = xr * hr - xi * hi
        pi = xr * hi + xi * hr
        pc = jnp.concatenate([pr, pi], axis=0).astype(BF16)
        c = lax.dot_general(gf_ref[i], pc, (((0,), (0,)), ((), ())), preferred_element_type=F32)
        c_ref[0, i] = c.astype(c_ref.dtype)


def _dft4_kernel(c_ref, f_ref, x0_ref, gv_ref, bias_ref, o_ref):
    y = _kron_dot(f_ref, c_ref[0, :, 0].astype(F32))
    o_ref[0, :, 0] = (x0_ref[0, :, 0] * (y + gv_ref[0, :, 0] * bias_ref[...])).astype(o_ref.dtype)


def _hyena(proj, tables, conv_w, conv_b, f_w1, f_b1, f_freq, f_w2, f_b2, f_w3, f_bias):
    bsz, seq, _ = proj.shape
    n = math.isqrt(2 * seq)
    assert n * n == 2 * seq and n % 2 == 0
    f1, f4, gfwd = tables
    ch = HY_W
    kb = min(8, n)
    x0c, gv = _hy_prep(proj, conv_w, conv_b)
    filt = _hy_filter(seq, f_w1, f_b1, f_freq, f_w2, f_b2, f_w3)

    a_f = _dft1(filt, f1, n).reshape(2, n, 2 * n, ch)
    adir = lambda bb: pl.BlockSpec((1, kb, 2 * n, ch), lambda k: (bb, k, 0, 0))
    gspec1 = pl.BlockSpec((kb, 2 * n, 2 * n), lambda k: (k, 0, 0))
    hspec1 = pl.BlockSpec((kb, n, ch), lambda k: (k, 0, 0))
    h_re, h_im = pl.pallas_call(
        _filt_spec_kernel,
        grid=(n // kb,),
        in_specs=[adir(0), adir(1), gspec1],
        out_specs=[hspec1, hspec1],
        out_shape=[jax.ShapeDtypeStruct((n, n, ch), F32)] * 2,
        compiler_params=_cp("parallel"),
        name="hyena_filter_spectrum",
    )(a_f, a_f, gfwd)

    a = _dft1(gv, f1, n).reshape(bsz, n, 2 * n, ch)
    aspec = pl.BlockSpec((1, kb, 2 * n, ch), lambda k, b: (b, k, 0, 0))
    gspec = pl.BlockSpec((kb, 2 * n, 2 * n), lambda k, b: (k, 0, 0))
    hspec = pl.BlockSpec((kb, n, ch), lambda k, b: (k, 0, 0))
    c = pl.pallas_call(
        _conv_spec_kernel,
        grid=(n // kb, bsz),
        in_specs=[aspec, gspec, hspec, hspec],
        out_specs=aspec,
        out_shape=jax.ShapeDtypeStruct((bsz, n, 2 * n, ch), BF16),
        compiler_params=_cp("parallel", "parallel"),
        name="hyena_conv_spectrum",
    )(a, gfwd, h_re, h_im).reshape(bsz, 2 * n, n // DFT_ROWS, DFT_ROWS, ch)

    nb = n // DFT_ROWS
    nat = lambda arr: arr.reshape(bsz, n // 2, nb, DFT_ROWS, ch)
    rspec = pl.BlockSpec((1, n // 2, 1, DFT_ROWS, ch), lambda b, j: (b, 0, j, 0, 0))
    out = pl.pallas_call(
        _dft4_kernel,
        grid=(bsz, nb),
        in_specs=[pl.BlockSpec((1, 2 * n, 1, DFT_ROWS, ch), lambda b, j: (b, 0, j, 0, 0)),
                  pl.BlockSpec(f4.shape, lambda b, j: (0, 0)),
                  rspec, rspec,
                  pl.BlockSpec((1, ch), lambda b, j: (0, 0))],
        out_specs=rspec,
        out_shape=jax.ShapeDtypeStruct((bsz, n // 2, nb, DFT_ROWS, ch), BF16),
        compiler_params=_cp("parallel", "parallel"),
        name="dft_stage4",
    )(c, f4, nat(x0c), nat(gv), f_bias.astype(F32).reshape(1, ch))
    return out.reshape(bsz, seq, ch)


def _outproj_kernel(om_ref, os_ref, oh_ref, x_ref, mod_ref, w_ref, g_ref, b_ref, o_ref):
    n_m, n_s = om_ref.shape[2], os_ref.shape[2]
    gate = mod_ref[0][2:3]
    half = x_ref.shape[1] // 2
    mixes = []
    for rows in (slice(0, half), slice(half, 2 * half)):
        mixed = jnp.dot(om_ref[0, rows], w_ref[0, 0:n_m], preferred_element_type=F32)
        mixed += jnp.dot(os_ref[0, rows], w_ref[0, n_m:n_m + n_s], preferred_element_type=F32)
        mixed += jnp.dot(oh_ref[0, rows], w_ref[0, n_m + n_s:], preferred_element_type=F32)
        mixes.append((rows, mixed))
    for rows, mixed in mixes:
        o_ref[0, rows] = _layer_norm(ALPHA * x_ref[0, rows] + gate * mixed, g_ref[...], b_ref[...])


def _outproj(o_mla, o_s5, o_hy, x, mod, w, l, ln_g, ln_b):
    bsz, seq, d = x.shape
    tm = min(512, seq)
    row = lambda width: pl.BlockSpec((1, tm, width), lambda b, i: (b, i, 0))
    const = lambda shp: pl.BlockSpec(shp, lambda b, i: (0,) * len(shp))
    return pl.pallas_call(
        _outproj_kernel,
        grid=(bsz, seq // tm),
        in_specs=[row(o_mla.shape[2]), row(o_s5.shape[2]), row(o_hy.shape[2]), row(d),
                  pl.BlockSpec((1, 6, d), lambda b, i: (b, 0, 0)),
                  pl.BlockSpec((1,) + w.shape[1:], lambda b, i: (l, 0, 0)), const((1, d)), const((1, d))],
        out_specs=row(d),
        out_shape=jax.ShapeDtypeStruct((bsz, seq, d), F32),
        compiler_params=_cp("parallel", "parallel"),
        name="out_proj_ln",
    )(o_mla, o_s5, o_hy, x, mod, w, ln_g.reshape(1, -1), ln_b.reshape(1, -1))


def _ffn_kernel(xp_ref, x_ref, xn_ref, mod_ref, wg_ref, wu_ref, cw_ref, cb_ref, wd_ref,
                g_ref, b_ref, o_ref, u_scr, acc_scr):
    i, j = pl.program_id(1), pl.program_id(2)
    tm = x_ref.shape[1]
    halo = xp_ref.shape[1]

    @pl.when(j == 0)
    def _():
        m = mod_ref[0]
        sc, sh = 1.0 + m[4:5], m[3:4]
        u_scr[0:halo] = (xp_ref[0] * sc + sh).astype(BF16)
        u_scr[halo:halo + tm] = (x_ref[0] * sc + sh).astype(BF16)
        u_scr[halo + tm:] = (xn_ref[0] * sc + sh).astype(BF16)
        acc_scr[...] = jnp.zeros_like(acc_scr)

    u = u_scr[...]
    rows = tm + 2 * halo
    gx = jnp.dot(u, wg_ref[0], preferred_element_type=F32)
    r = lax.broadcasted_iota(jnp.int32, (tm, 1), 0)
    keep_prev = jnp.logical_or(r > 0, i > 0)
    keep_next = jnp.logical_or(r < tm - 1, i < pl.num_programs(1) - 1)
    g_prev = jnp.where(keep_prev, pltpu.roll(gx, 1, 0)[halo:halo + tm], 0.0)
    g_next = jnp.where(keep_next, pltpu.roll(gx, rows - 1, 0)[halo:halo + tm], 0.0)
    cw = cw_ref[...]
    conv = g_prev * cw[0:1] + gx[halo:halo + tm] * cw[1:2] + g_next * cw[2:3] + cb_ref[...]
    up = jnp.dot(u[halo:halo + tm], wu_ref[0], preferred_element_type=F32)
    h = (conv * jax.nn.sigmoid(conv) * up).astype(BF16)
    acc_scr[...] += jnp.dot(h, wd_ref[0], preferred_element_type=F32)

    @pl.when(j == pl.num_programs(2) - 1)
    def _():
        gate = mod_ref[0][5:6]
        o_ref[0] = _layer_norm(ALPHA * x_ref[0] + gate * acc_scr[...], g_ref[...], b_ref[...])


def _ffn(x, mod, wg, wu, conv_w, conv_b, wd, l, ln_g, ln_b):
    bsz, seq, d = x.shape
    ff = wg.shape[2]
    tm = min(512, seq)
    tn = 512
    halo = SUBLANE
    nhb = seq // halo
    per = tm // halo
    const = lambda shp: pl.BlockSpec(shp, lambda b, i, j: (0,) * len(shp))
    return pl.pallas_call(
        _ffn_kernel,
        grid=(bsz, seq // tm, ff // tn),
        in_specs=[pl.BlockSpec((1, halo, d), lambda b, i, j: (b, jnp.maximum(i * per - 1, 0), 0)),
                  pl.BlockSpec((1, tm, d), lambda b, i, j: (b, i, 0)),
                  pl.BlockSpec((1, halo, d), lambda b, i, j: (b, jnp.minimum((i + 1) * per, nhb - 1), 0)),
                  pl.BlockSpec((1, 6, d), lambda b, i, j: (b, 0, 0)),
                  pl.BlockSpec((1, d, tn), lambda b, i, j: (l, 0, j)),
                  pl.BlockSpec((1, d, tn), lambda b, i, j: (l, 0, j)),
                  pl.BlockSpec((3, tn), lambda b, i, j: (0, j)),
                  pl.BlockSpec((1, tn), lambda b, i, j: (0, j)),
                  pl.BlockSpec((1, tn, d), lambda b, i, j: (l, j, 0)),
                  const((1, d)), const((1, d))],
        out_specs=pl.BlockSpec((1, tm, d), lambda b, i, j: (b, i, 0)),
        out_shape=jax.ShapeDtypeStruct((bsz, seq, d), F32),
        scratch_shapes=[pltpu.VMEM((tm + 2 * halo, d), BF16), pltpu.VMEM((tm, d), F32)],
        compiler_params=_cp("parallel", "parallel", "arbitrary"),
        name="conv_ffn_ln",
    )(x, x, x, mod, wg, wu, conv_w.astype(F32), conv_b.reshape(1, -1).astype(F32), wd,
      ln_g.reshape(1, -1), ln_b.reshape(1, -1))


def _rot_half_cols(w):
    half = w.shape[-1] // 2
    return jnp.concatenate([-w[..., half:], w[..., :half]], axis=-1)


def _prep_w_in(w):
    q, kv, kr, s5, hy = jnp.split(w, (512, 768, 832, 1344), axis=-1)
    return jnp.concatenate([q, hy, kv, kr, _rot_half_cols(kr), s5], axis=-1).astype(BF16)


def _prep_w_uq(w):
    w = w.reshape(w.shape[:-1] + (MLA_HEADS, MLA_QK))
    pe = w[..., MLA_NOPE:]
    return jnp.concatenate([w, _rot_half_cols(pe)], axis=-1).reshape(w.shape[:-2] + (-1,)).astype(BF16)


def kernel(x, c, positions, ada_w, ada_b, w_in, q_norm_g, kv_norm_g, w_uq, w_ukv, s5_a_re, s5_a_im, s5_log_step, s5_b_re, s5_b_im, s5_c_re, s5_c_im, s5_d, s5_glu_w, s5_glu_b, hy_conv_w, hy_conv_b, hy_f_w1, hy_f_b1, hy_f_freq, hy_f_w2, hy_f_b2, hy_f_w3, hy_f_bias, w_out, ln1_g, ln1_b, ffn_w_gate, ffn_w_up, ffn_conv_w, ffn_conv_b, ffn_w_down, ln2_g, ln2_b):
    bsz, seq, _ = x.shape
    depth = ada_w.shape[0]
    mods = _ada(c, ada_w, ada_b)
    tables = _dft_tables(math.isqrt(2 * seq))
    w_in_b = _prep_w_in(w_in)
    w_uq_b = _prep_w_uq(w_uq)
    w_ukv_b = w_ukv.astype(BF16)
    w_out_b = w_out.astype(BF16)
    wg_b, wu_b, wd_b = ffn_w_gate.astype(BF16), ffn_w_up.astype(BF16), ffn_w_down.astype(BF16)
    s5_all = jax.vmap(_s5_params)(s5_a_re, s5_a_im, s5_log_step, s5_b_re, s5_b_im, s5_c_re, s5_c_im, s5_d)
    for l in range(depth):
        mod = mods[l]
        proj, xs5 = _inproj(x, mod, w_in_b, l)
        q, k, v = _mla_prep(proj, positions, q_norm_g[l], kv_norm_g[l], w_uq_b[l], w_ukv_b[l])
        o_mla = _attention(q, k, v)
        o_s5 = _s5(xs5, s5_all, l, s5_glu_w[l], s5_glu_b[l])
        o_hy = _hyena(proj, tables, hy_conv_w[l], hy_conv_b[l], hy_f_w1[l], hy_f_b1[l],
                      hy_f_freq[l], hy_f_w2[l], hy_f_b2[l], hy_f_w3[l], hy_f_bias[l])
        x = _outproj(o_mla, o_s5, o_hy, x, mod, w_out_b, l, ln1_g[l], ln1_b[l])
        x = _ffn(x, mod, wg_b, wu_b, ffn_conv_w[l], ffn_conv_b[l], wd_b, l, ln2_g[l], ln2_b[l])
    return x
```

```python
import functools
import math

import numpy as np
import jax
import jax.numpy as jnp
from jax import lax
from jax.experimental import pallas as pl
from jax.experimental.pallas import tpu as pltpu

F32 = jnp.float32
BF16 = jnp.bfloat16

MLA_HEADS = 8
MLA_NOPE = 128
MLA_ROPE = 64
MLA_V = 128
MLA_QK = MLA_NOPE + MLA_ROPE
MLA_Q_RANK = 512
MLA_KV_RANK = 256
ROPE_THETA = 10000.0
S5_W = 512
S5_GROUP = 16
S5_GROUPS = 32
S5_STATE = 64
S5_CHUNK = 16
S5_OCT = 8
HY_W = 512
HY_BANDS = 16
HY_DECAY_TARGET = 1e-2
HY_FAST_PCT = 0.3
HY_SLOW_PCT = 1.5
LN_EPS = 1e-5
RMS_EPS = 1e-6
FILTER_EPS = 1e-6
DEPTH = 2
ALPHA = (2 * DEPTH) ** 0.25

COL_Q = 0
COL_HY = 512
COL_KV = 2048
COL_KR = 2304
COL_S5 = 2432
IN_COLS_PAD = 2944

LANE = 128
SUBLANE = 8
VMEM_LIMIT = 56 * 1024 * 1024


def _cp(*sem):
    return pltpu.CompilerParams(dimension_semantics=sem, vmem_limit_bytes=VMEM_LIMIT)


def _layer_norm(y, g, b):
    mu = jnp.mean(y, axis=-1, keepdims=True)
    d = y - mu
    var = jnp.mean(d * d, axis=-1, keepdims=True)
    return d * lax.rsqrt(var + LN_EPS) * g + b


def _ada_kernel(c_ref, w_ref, b_ref, o_ref):
    c = c_ref[...]
    cond = c * jax.nn.sigmoid(c)
    o_ref[0] = jnp.dot(cond, w_ref[0], preferred_element_type=F32) + b_ref[0]


def _ada(c, ada_w, ada_b):
    bsz, d = c.shape
    depth, _, n = ada_w.shape
    tn = 1024
    cp = jnp.zeros((SUBLANE, d), F32).at[:bsz].set(c)
    out = pl.pallas_call(
        _ada_kernel,
        grid=(depth, n // tn),
        in_specs=[pl.BlockSpec((SUBLANE, d), lambda l, j: (0, 0)),
                  pl.BlockSpec((1, d, tn), lambda l, j: (l, 0, j)),
                  pl.BlockSpec((1, 1, tn), lambda l, j: (l, 0, j))],
        out_specs=pl.BlockSpec((1, SUBLANE, tn), lambda l, j: (l, 0, j)),
        out_shape=jax.ShapeDtypeStruct((depth, SUBLANE, n), F32),
        compiler_params=_cp("parallel", "parallel"),
        name="ada_mod",
    )(cp, ada_w, ada_b.reshape(depth, 1, n))
    return out[:, :bsz].reshape(depth, bsz, 6, d)


def _chunk_perm(chunks, to_time_major):
    q = S5_CHUNK
    nat = np.arange(chunks * q).reshape(chunks, q)
    tmaj = nat.T.reshape(-1)
    p = np.zeros((chunks * q, chunks * q), np.float32)
    if to_time_major:
        p[np.arange(chunks * q), tmaj] = 1.0
    else:
        p[tmaj, np.arange(chunks * q)] = 1.0
    return jnp.asarray(p, BF16)


def _inproj_kernel(x_ref, mod_ref, w_ref, perm_ref, o_ref, xs_ref):
    m = mod_ref[0]
    u = x_ref[0] * (1.0 + m[1:2]) + m[0:1]
    res = jnp.dot(u.astype(BF16), w_ref[0], preferred_element_type=F32)
    o_ref[0] = res[:, :COL_S5]
    us = jnp.dot(perm_ref[...], res[:, COL_S5:].astype(BF16), preferred_element_type=F32).astype(BF16)
    cn = xs_ref.shape[1]
    oct_w = S5_CHUNK * LANE
    for t in range(S5_CHUNK):
        for o in range(S5_W // LANE):
            xs_ref[0, :, o * oct_w + t * LANE:o * oct_w + (t + 1) * LANE] = (
                us[t * cn:(t + 1) * cn, o * LANE:(o + 1) * LANE])


def _inproj(x, mod, w, l):
    bsz, seq, d = x.shape
    n = w.shape[2]
    tm = min(256, seq)
    cn = tm // S5_CHUNK
    return pl.pallas_call(
        _inproj_kernel,
        grid=(bsz, seq // tm),
        in_specs=[pl.BlockSpec((1, tm, d), lambda b, i: (b, i, 0)),
                  pl.BlockSpec((1, 6, d), lambda b, i: (b, 0, 0)),
                  pl.BlockSpec((1, d, n), lambda b, i: (l, 0, 0)),
                  pl.BlockSpec((tm, tm), lambda b, i: (0, 0))],
        out_specs=[pl.BlockSpec((1, tm, COL_S5), lambda b, i: (b, i, 0)),
                   pl.BlockSpec((1, cn, S5_CHUNK * S5_W), lambda b, i: (b, i, 0))],
        out_shape=[jax.ShapeDtypeStruct((bsz, seq, COL_S5), F32),
                   jax.ShapeDtypeStruct((bsz, seq // S5_CHUNK, S5_CHUNK * S5_W), BF16)],
        compiler_params=_cp("parallel", "parallel"),
        name="in_proj",
    )(x, mod, w, _chunk_perm(cn, True))


def _rope_trig(ang):
    fw = MLA_ROPE // 2
    nb = LANE // fw
    rq = ang.shape[0] // nb
    blk = lax.broadcasted_iota(jnp.int32, (rq, LANE), 1) // fw
    packed = ang[0:rq]
    for b in range(1, nb):
        packed = jnp.where(blk == b, ang[b * rq:(b + 1) * rq], packed)
    outs = []
    for t in (jnp.cos(packed), jnp.sin(packed)):
        rows = []
        for b in range(nb):
            one = jnp.where(blk == b, t, 0.0)
            full = one
            for r in range(1, nb):
                full = full + pltpu.roll(one, r * fw, 1)
            rows.append(full)
        outs.append(jnp.concatenate(rows, axis=0))
    return outs


def _mla_prep_kernel(ql_ref, kvl_ref, kr_ref, pos_ref, invf_ref, qg_ref, kvg_ref,
                     wq_ref, wkv_ref, q_ref, k_ref, v_ref):
    scale = MLA_QK ** -0.5 * math.log2(math.e)
    ql = ql_ref[0]
    qn = ql * lax.rsqrt(jnp.mean(ql * ql, axis=-1, keepdims=True) + RMS_EPS) * qg_ref[...]
    kvl = kvl_ref[0]
    kvn = kvl * lax.rsqrt(jnp.mean(kvl * kvl, axis=-1, keepdims=True) + RMS_EPS) * kvg_ref[...]
    ang = pos_ref[0] * invf_ref[...]
    cos, sin = _rope_trig(ang)
    qa = jnp.dot(qn.astype(BF16), wq_ref[...], preferred_element_type=F32)
    kva = jnp.dot(kvn.astype(BF16), wkv_ref[...], preferred_element_type=F32)
    kr = kr_ref[0]
    kpe = (kr * cos + pltpu.roll(kr, MLA_ROPE, 1) * sin)[:, :MLA_ROPE].astype(BF16)
    for h in range(MLA_HEADS):
        c0 = 2 * LANE * h
        pr = qa[:, c0 + LANE:c0 + 2 * LANE]
        qpe = pr * cos + pltpu.roll(pr, MLA_ROPE, 1) * sin
        q_ref[0, h, :, 0:MLA_NOPE] = (qa[:, c0:c0 + LANE] * scale).astype(BF16)
        q_ref[0, h, :, MLA_NOPE:MLA_QK] = (qpe[:, :MLA_ROPE] * scale).astype(BF16)
        k_ref[0, h, :, 0:MLA_NOPE] = kva[:, c0:c0 + LANE].astype(BF16)
        k_ref[0, h, :, MLA_NOPE:MLA_QK] = kpe
        v_ref[0, h] = kva[:, c0 + LANE:c0 + 2 * LANE].astype(BF16)


def _mla_prep(proj, positions, q_g, kv_g, wq, wkv):
    bsz, seq, _ = proj.shape
    tm = min(512, seq)
    inv_freq = ROPE_THETA ** (-jnp.arange(0, MLA_ROPE, 2, dtype=F32) / MLA_ROPE)
    invf = jnp.tile(inv_freq, LANE // (MLA_ROPE // 2)).reshape(1, LANE)
    pos = positions.astype(F32).reshape(bsz, seq, 1)
    hshape = lambda w: jax.ShapeDtypeStruct((bsz, MLA_HEADS, seq, w), BF16)
    hspec = lambda w: pl.BlockSpec((1, MLA_HEADS, tm, w), lambda b, i: (b, 0, i, 0))
    const = lambda shp: pl.BlockSpec(shp, lambda b, i: (0,) * len(shp))
    return pl.pallas_call(
        _mla_prep_kernel,
        grid=(bsz, seq // tm),
        in_specs=[pl.BlockSpec((1, tm, MLA_Q_RANK), lambda b, i: (b, i, COL_Q // MLA_Q_RANK)),
                  pl.BlockSpec((1, tm, MLA_KV_RANK), lambda b, i: (b, i, COL_KV // MLA_KV_RANK)),
                  pl.BlockSpec((1, tm, LANE), lambda b, i: (b, i, COL_KR // LANE)),
                  pl.BlockSpec((1, tm, 1), lambda b, i: (b, i, 0)),
                  const((1, LANE)), const((1, MLA_Q_RANK)), const((1, MLA_KV_RANK)),
                  const(wq.shape), const(wkv.shape)],
        out_specs=[hspec(MLA_QK), hspec(MLA_QK), hspec(MLA_V)],
        out_shape=[hshape(MLA_QK), hshape(MLA_QK), hshape(MLA_V)],
        compiler_params=_cp("parallel", "parallel"),
        name="mla_prep",
    )(proj, proj, proj, pos, invf, q_g.reshape(1, -1), kv_g.reshape(1, -1), wq, wkv)


def _attn_kernel(q_ref, k_ref, v_ref, o_ref, *, tk):
    q = q_ref[0, 0]
    tq = q.shape[0]
    nk = k_ref.shape[2] // tk

    def body(j, carry):
        m, l, acc = carry
        off = pl.multiple_of(j * tk, tk)
        kj = k_ref[0, 0, pl.ds(off, tk), :]
        vj = v_ref[0, 0, pl.ds(off, tk), :]
        s = lax.dot_general(q, kj, (((1,), (1,)), ((), ())), preferred_element_type=F32)
        m_new = jnp.maximum(m, jnp.max(s, axis=-1, keepdims=True))
        a = jnp.exp2(m - m_new)
        p = jnp.exp2(s - m_new)
        l = a * l + jnp.sum(p, axis=-1, keepdims=True)
        acc = a * acc + jnp.dot(p.astype(BF16), vj, preferred_element_type=F32)
        return m_new, l, acc

    init = (jnp.full((tq, 1), jnp.finfo(F32).min, F32), jnp.zeros((tq, 1), F32),
            jnp.zeros((tq, MLA_V), F32))
    _, l, acc = lax.fori_loop(0, nk, body, init, unroll=True)
    o_ref[0] = (acc / l).astype(o_ref.dtype)


def _attention(q, k, v):
    bsz, nh, seq, _ = q.shape
    tq = min(1024, seq)
    tk = min(2048, seq)
    return pl.pallas_call(
        functools.partial(_attn_kernel, tk=tk),
        grid=(bsz, nh, seq // tq),
        in_specs=[pl.BlockSpec((1, 1, tq, MLA_QK), lambda b, h, i: (b, h, i, 0)),
                  pl.BlockSpec((1, 1, seq, MLA_QK), lambda b, h, i: (b, h, 0, 0)),
                  pl.BlockSpec((1, 1, seq, MLA_V), lambda b, h, i: (b, h, 0, 0))],
        out_specs=pl.BlockSpec((1, tq, MLA_V), lambda b, h, i: (b, i, h)),
        out_shape=jax.ShapeDtypeStruct((bsz, seq, nh * MLA_V), BF16),
        compiler_params=_cp("parallel", "parallel", "parallel"),
        name="mla_attention",
    )(q, k, v)


def _s5_params(a_re, a_im, log_step, b_re, b_im, c_re, c_im, d):
    q = S5_CHUNK
    g, p, hh = S5_GROUPS, S5_STATE, S5_GROUP
    j = S5_OCT
    no = g // j
    ks = jnp.arange(q + 1, dtype=F32)[:, None, None]
    out = {}
    state_tabs, out_tabs, c0_tabs = [], [], []

    def oct_rows(w):
        t = w.shape[1]
        return w.reshape(no, j, t, hh, p).transpose(0, 2, 1, 3, 4).reshape(no, t * j * hh, p)

    for direction in range(2):
        ar, ai = a_re[direction].astype(F32), a_im[direction].astype(F32)
        step = jnp.exp(log_step[direction].astype(F32))[:, None]
        mag = jnp.exp(step * ar)
        abar_r = mag * jnp.cos(step * ai)
        abar_i = mag * jnp.sin(step * ai)
        den = ar * ar + ai * ai
        nr = abar_r - 1.0
        fr = (nr * ar + abar_i * ai) / den
        fi = (abar_i * ar - nr * ai) / den
        pmag = jnp.exp(ks * (step * ar)[None])
        pw_r = pmag * jnp.cos(ks * (step * ai)[None])
        pw_i = pmag * jnp.sin(ks * (step * ai)[None])
        br, bi = b_re[direction].astype(F32), b_im[direction].astype(F32)
        bf_r = fr[..., None] * br - fi[..., None] * bi
        bf_i = fr[..., None] * bi + fi[..., None] * br
        cr, ci = c_re[direction].astype(F32), c_im[direction].astype(F32)
        bt_r, bt_i = bf_r.transpose(0, 2, 1)[:, None], bf_i.transpose(0, 2, 1)[:, None]
        tpow = (q - 1 - jnp.arange(q)) if direction == 0 else jnp.arange(q)
        sr = pw_r[tpow].transpose(1, 0, 2)[:, :, None, :]
        si = pw_i[tpow].transpose(1, 0, 2)[:, :, None, :]
        state_tabs += [oct_rows(sr * bt_r - si * bt_i), oct_rows(sr * bt_i + si * bt_r)]
        opow = (jnp.arange(q) + 1) if direction == 0 else (q - jnp.arange(q))
        orr = pw_r[opow].transpose(1, 0, 2)[:, :, None, :]
        oi = pw_i[opow].transpose(1, 0, 2)[:, :, None, :]
        out_tabs += [oct_rows(cr[:, None] * orr - ci[:, None] * oi),
                     oct_rows(-(cr[:, None] * oi + ci[:, None] * orr))]
        c0_tabs += [oct_rows(cr[:, None]), oct_rows(-ci[:, None])]
        out[f'a_r{direction}'] = pw_r[q].reshape(1, g * p)
        out[f'a_i{direction}'] = pw_i[q].reshape(1, g * p)
    out['tab'] = jnp.stack(state_tabs + out_tabs).astype(BF16)
    out['c0'] = jnp.stack(c0_tabs).astype(BF16)
    out['dvec'] = d.astype(F32).reshape(no, 1, j * hh)
    return out


def _expand_const():
    p, j = S5_STATE, S5_OCT
    return jnp.asarray(np.tile(np.eye(p, dtype=np.float32), (1, j)), BF16)


def _expand(block, e_ref, same):
    w = jnp.dot(block, e_ref[...], preferred_element_type=F32)
    return jnp.where(same, w, 0.0).astype(BF16)


def _same_group_mask():
    hh, p, j = S5_GROUP, S5_STATE, S5_OCT
    shape = (j * hh, j * p)
    return (lax.broadcasted_iota(jnp.int32, shape, 0) // hh) == (lax.broadcasted_iota(jnp.int32, shape, 1) // p)


def _s5_state_kernel(x_ref, tab_ref, e_ref, vfr_ref, vfi_ref, vbr_ref, vbi_ref, w_scr):
    q, bw, sw = S5_CHUNK, S5_OCT * S5_GROUP, S5_OCT * S5_STATE

    @pl.when(pl.program_id(1) == 0)
    def _():
        same = _same_group_mask()
        for t in range(q):
            rows = slice(t * bw, (t + 1) * bw)
            for part in range(4):
                w_scr[rows, part * sw:(part + 1) * sw] = _expand(tab_ref[0, part, 0, rows, :], e_ref, same)

    v = jnp.dot(x_ref[...], w_scr[...], preferred_element_type=F32)
    w = vfr_ref.shape[1]
    vfr_ref[...] = v[:, 0 * w:1 * w]
    vfi_ref[...] = v[:, 1 * w:2 * w]
    vbr_ref[...] = v[:, 2 * w:3 * w]
    vbi_ref[...] = v[:, 3 * w:4 * w]


def _s5_scan_kernel(vfr_ref, vfi_ref, vbr_ref, vbi_ref, afr_ref, afi_ref, abr_ref, abi_ref,
                    sfr_ref, sfi_ref, sbr_ref, sbi_ref):
    nb, nc = vfr_ref.shape[0], vfr_ref.shape[1]
    afr, afi, abr, abi = afr_ref[0], afi_ref[0], abr_ref[0], abi_ref[0]

    def body(c, carry):
        cb = nc - 1 - c
        new = []
        for b in range(nb):
            fr, fi, br, bi = carry[4 * b:4 * b + 4]
            sfr_ref[b, pl.ds(c, 1), :] = fr
            sfi_ref[b, pl.ds(c, 1), :] = fi
            sbr_ref[b, pl.ds(cb, 1), :] = br
            sbi_ref[b, pl.ds(cb, 1), :] = bi
            new += [afr * fr - afi * fi + vfr_ref[b, pl.ds(c, 1), :],
                    afr * fi + afi * fr + vfi_ref[b, pl.ds(c, 1), :],
                    abr * br - abi * bi + vbr_ref[b, pl.ds(cb, 1), :],
                    abr * bi + abi * br + vbi_ref[b, pl.ds(cb, 1), :]]
        return tuple(new)

    z = jnp.zeros((1, vfr_ref.shape[2]), F32)
    lax.fori_loop(0, nc, body, (z,) * (4 * nb))


def _dot_nt(a, b):
    return lax.dot_general(a, b, (((1,), (1,)), ((), ())), preferred_element_type=F32)


def _s5_out_kernel(x_ref, tabs_ref, tabo_ref, c0_ref, d_ref, sfr_ref, sfi_ref, sbr_ref, sbi_ref, e_ref,
                   y_ref, m_scr, cp_scr):
    q, bw = S5_CHUNK, S5_OCT * S5_GROUP

    @pl.when(pl.program_id(1) == 0)
    def _():
        same = _same_group_mask()
        c0 = [_expand(c0_ref[0, part, 0], e_ref, same) for part in range(4)]
        lag_f, lag_b = [None] * q, [None] * q
        for t in range(q):
            rows = slice(t * bw, (t + 1) * bw)
            for part in range(4):
                cp_scr[part, rows, :] = _expand(tabo_ref[0, part, 0, rows, :], e_ref, same)
            ws = [_expand(tabs_ref[0, part, 0, rows, :], e_ref, same) for part in range(4)]
            lag_f[q - 1 - t] = _dot_nt(ws[0], c0[0]) + _dot_nt(ws[1], c0[1])
            lag_b[t] = _dot_nt(ws[2], c0[2]) + _dot_nt(ws[3], c0[3])
        eye = (lax.broadcasted_iota(jnp.int32, (bw, bw), 0) == lax.broadcasted_iota(jnp.int32, (bw, bw), 1))
        lag0 = (lag_f[0] + lag_b[0] + jnp.where(eye, d_ref[0, 0], 0.0)).astype(BF16)
        lag_f = [v.astype(BF16) for v in lag_f]
        lag_b = [v.astype(BF16) for v in lag_b]
        for ti in range(q):
            for to in range(q):
                blk = lag0 if to == ti else (lag_f[to - ti] if to > ti else lag_b[ti - to])
                m_scr[ti * bw:(ti + 1) * bw, to * bw:(to + 1) * bw] = blk

    y = jnp.dot(x_ref[...], m_scr[...], preferred_element_type=F32)
    for part, s_ref in enumerate((sfr_ref, sfi_ref, sbr_ref, sbi_ref)):
        y += _dot_nt(s_ref[...].astype(BF16), cp_scr[part])
    y_ref[...] = y


def _s5_post_kernel(y_ref, w_ref, b_ref, perm_ref, o_ref):
    oct_w = S5_CHUNK * LANE
    y = jnp.concatenate(
        [jnp.concatenate([y_ref[:, o * oct_w + t * LANE:o * oct_w + (t + 1) * LANE]
                          for o in range(S5_W // LANE)], axis=1)
         for t in range(S5_CHUNK)], axis=0)
    y = jax.nn.gelu(y)
    z = jnp.dot(y.astype(BF16), w_ref[...], preferred_element_type=F32) + b_ref[...]
    o = (y * jax.nn.sigmoid(z)).astype(BF16)
    o_ref[...] = jnp.dot(perm_ref[...], o, preferred_element_type=F32).astype(o_ref.dtype)


def _s5(xs, prm, l, glu_w, glu_b):
    bsz, nc, _ = xs.shape
    q, g, p = S5_CHUNK, S5_GROUPS, S5_STATE
    seq = nc * q
    no = g // S5_OCT
    rows = bsz * nc
    feat = q * LANE
    sw = S5_OCT * p
    nst = g * p
    x = xs.reshape(rows, no * feat)
    e = _expand_const()
    tab = prm['tab']
    tab_spec = lambda half: pl.BlockSpec((1, 4, 1, feat, p), lambda o, i: (l, half, o, 0, 0))
    espec = pl.BlockSpec(e.shape, lambda o, i: (0, 0))
    rt = min(512, rows)
    xspec = pl.BlockSpec((rt, feat), lambda o, i: (i, o))
    vspec = pl.BlockSpec((rt, sw), lambda o, i: (i, o))
    vshape = jax.ShapeDtypeStruct((rows, nst), F32)
    v4 = pl.pallas_call(
        _s5_state_kernel,
        grid=(no, rows // rt),
        in_specs=[xspec, tab_spec(0), espec],
        out_specs=[vspec] * 4,
        out_shape=[vshape] * 4,
        scratch_shapes=[pltpu.VMEM((feat, 4 * sw), BF16)],
        compiler_params=_cp("parallel", "arbitrary"),
        name="s5_state",
    )(x, tab, e)

    lt = min(512, nst)
    v4 = [v.reshape(bsz, nc, nst) for v in v4]
    sspec = pl.BlockSpec((bsz, nc, lt), lambda n: (0, 0, n))
    aspec = pl.BlockSpec((1, 1, lt), lambda n: (l, 0, n))
    s4 = pl.pallas_call(
        _s5_scan_kernel,
        grid=(nst // lt,),
        in_specs=[sspec] * 4 + [aspec] * 4,
        out_specs=[sspec] * 4,
        out_shape=[jax.ShapeDtypeStruct((bsz, nc, nst), F32)] * 4,
        compiler_params=_cp("parallel"),
        name="s5_scan",
    )(*v4, prm['a_r0'], prm['a_i0'], prm['a_r1'], prm['a_i1'])
    s4 = [s.reshape(rows, nst) for s in s4]

    ro = min(256, rows)
    xspec = pl.BlockSpec((ro, feat), lambda o, i: (i, o))
    vspec = pl.BlockSpec((ro, sw), lambda o, i: (i, o))
    bw = S5_OCT * S5_GROUP
    y = pl.pallas_call(
        _s5_out_kernel,
        grid=(no, rows // ro),
        in_specs=([xspec, tab_spec(0), tab_spec(1),
                   pl.BlockSpec((1, 4, 1, bw, p), lambda o, i: (l, 0, o, 0, 0)),
                   pl.BlockSpec((1, 1, 1, bw), lambda o, i: (l, o, 0, 0))] + [vspec] * 4 + [espec]),
        out_specs=xspec,
        out_shape=jax.ShapeDtypeStruct((rows, no * feat), F32),
        scratch_shapes=[pltpu.VMEM((feat, feat), BF16), pltpu.VMEM((4, feat, sw), BF16)],
        compiler_params=_cp("parallel", "arbitrary"),
        name="s5_out",
    )(x, tab, tab, prm['c0'], prm['dvec'], *s4, e)

    cn = min(32, rows)
    const = lambda shp: pl.BlockSpec(shp, lambda i: (0,) * len(shp))
    out = pl.pallas_call(
        _s5_post_kernel,
        grid=(rows // cn,),
        in_specs=[pl.BlockSpec((cn, no * feat), lambda i: (i, 0)),
                  const((S5_W, S5_W)), const((1, S5_W)), const((cn * q, cn * q))],
        out_specs=pl.BlockSpec((cn * q, S5_W), lambda i: (i, 0)),
        out_shape=jax.ShapeDtypeStruct((rows * q, S5_W), BF16),
        compiler_params=_cp("parallel"),
        name="s5_post",
    )(y, glu_w.astype(BF16), glu_b.reshape(1, -1).astype(F32), _chunk_perm(cn, False))
    return out.reshape(bsz, seq, S5_W)


def _hy_prep_kernel(x0_ref, x1_ref, v_ref, w_ref, b_ref, x0c_ref, gv_ref):
    seq = x0_ref.shape[1]
    row = lax.broadcasted_iota(jnp.int32, (seq, LANE), 0)
    first = row == 0
    last = row == seq - 1

    def conv(ref, k):
        u = ref[0]
        w = w_ref[k]
        prev = jnp.where(first, 0.0, pltpu.roll(u, 1, 0))
        nxt = jnp.where(last, 0.0, pltpu.roll(u, seq - 1, 0))
        return prev * w[0:1] + u * w[1:2] + nxt * w[2:3] + b_ref[k]

    x0c_ref[0] = conv(x0_ref, 0)
    gv_ref[0] = conv(x1_ref, 1) * conv(v_ref, 2)


def _hy_prep(proj, conv_w, conv_b):
    bsz, seq, _ = proj.shape
    nt = HY_W // LANE
    base = COL_HY // LANE
    w = conv_w.astype(F32).reshape(3, 3, HY_W).transpose(1, 0, 2)
    b = conv_b.astype(F32).reshape(3, 1, HY_W)
    col = lambda part: pl.BlockSpec((1, seq, LANE), lambda bb, c: (bb, 0, base + part * nt + c))
    ospec = pl.BlockSpec((1, seq, LANE), lambda bb, c: (bb, 0, c))
    return pl.pallas_call(
        _hy_prep_kernel,
        grid=(bsz, nt),
        in_specs=[col(0), col(1), col(2),
                  pl.BlockSpec((3, 3, LANE), lambda bb, c: (0, 0, c)),
                  pl.BlockSpec((3, 1, LANE), lambda bb, c: (0, 0, c))],
        out_specs=[ospec, ospec],
        out_shape=[jax.ShapeDtypeStruct((bsz, seq, HY_W), F32)] * 2,
        compiler_params=_cp("parallel", "parallel"),
        name="hyena_prep",
    )(proj, proj, proj, w, b)


def _hy_filter_kernel(feat_ref, w1_ref, b1_ref, fq_ref, w2_ref, b2_ref, w3f_ref, w3b_ref,
                      dl_ref, o_ref, h_scr):
    @pl.when(pl.program_id(0) == 0)
    def _():
        fq = fq_ref[...]
        h1 = jnp.sin(fq[0:1] * (jnp.dot(feat_ref[...], w1_ref[...], preferred_element_type=F32) + b1_ref[...]))
        h_scr[...] = jnp.sin(fq[1:2] * (jnp.dot(h1, w2_ref[...], preferred_element_type=F32) + b2_ref[...]))

    h = h_scr[...]
    win = jnp.exp(-feat_ref[:, 0:1] * dl_ref[...])
    hq = h.astype(BF16)
    hf = jnp.dot(hq, w3f_ref[...].astype(BF16), preferred_element_type=F32) * win
    hb = jnp.dot(hq, w3b_ref[...].astype(BF16), preferred_element_type=F32) * win
    row = lax.broadcasted_iota(jnp.int32, hb.shape, 0)
    hb = jnp.where(row == 0, 0.0, hb)
    ss = jnp.sum(hf * hf, axis=0, keepdims=True) + jnp.sum(hb * hb, axis=0, keepdims=True)
    scale = lax.rsqrt(ss + FILTER_EPS)
    o_ref[0] = hf * scale
    o_ref[1] = hb * scale


def _hy_filter(seq, w1, b1, freq, w2, b2, w3):
    t01 = jnp.linspace(0.0, 1.0, seq, dtype=F32)[:, None]
    w = 2.0 * math.pi * jnp.arange(seq, dtype=F32)[:, None] / seq
    bands = jnp.linspace(1e-4, HY_BANDS - 1, HY_BANDS, dtype=F32)[None, :]
    fw = w * bands
    emb = 1 + 2 * HY_BANDS
    kpad = 64
    feat = jnp.concatenate([t01, jnp.cos(fw), -jnp.sin(fw), jnp.zeros((seq, kpad - emb), F32)], axis=-1)
    w1p = jnp.zeros((kpad, w1.shape[1]), F32).at[:emb].set(w1.astype(F32))
    deltas = jnp.abs(jnp.linspace(math.log(HY_DECAY_TARGET) / HY_SLOW_PCT,
                                  math.log(HY_DECAY_TARGET) / HY_FAST_PCT, HY_W, dtype=F32))[None, :]
    nt = HY_W // LANE
    hid = w2.shape[0]
    const = lambda shp: pl.BlockSpec(shp, lambda c: (0,) * len(shp))
    return pl.pallas_call(
        _hy_filter_kernel,
        grid=(nt,),
        in_specs=[const((seq, kpad)), const((kpad, hid)), const((1, hid)), const((2, hid)),
                  const((hid, hid)), const((1, hid)),
                  pl.BlockSpec((hid, LANE), lambda c: (0, c)),
                  pl.BlockSpec((hid, LANE), lambda c: (0, nt + c)),
                  pl.BlockSpec((1, LANE), lambda c: (0, c))],
        out_specs=pl.BlockSpec((2, seq, LANE), lambda c: (0, 0, c)),
        out_shape=jax.ShapeDtypeStruct((2, seq, HY_W), F32),
        scratch_shapes=[pltpu.VMEM((seq, hid), F32)],
        compiler_params=_cp("arbitrary"),
        name="hyena_filter",
    )(feat, w1p, b1.reshape(1, -1).astype(F32), freq.astype(F32), w2.astype(F32),
      b2.reshape(1, -1).astype(F32), w3.astype(F32), w3.astype(F32), deltas)


def _dft_tables(n):
    big = n * n
    kk = np.arange(n)
    ph1 = (2.0 * np.pi / n) * ((kk[:, None] * kk[None, :n // 2]) % n)
    f1 = np.stack([np.cos(ph1), -np.sin(ph1)], axis=1).reshape(2 * n, n // 2)
    eye = np.eye(SUBLANE)
    f4 = jnp.asarray(np.kron(f1.T / big, eye).astype(np.float32), BF16)
    f1 = jnp.asarray(np.kron(f1, eye).astype(np.float32), BF16)
    idx = (kk[None, None, :] * (kk[:, None, None] + n * kk[None, :, None])) % big
    th = (2.0 * np.pi / big) * idx
    gr, gi = np.cos(th), -np.sin(th)
    gfwd = np.concatenate([np.concatenate([gr, -gi], axis=2),
                           np.concatenate([gi, gr], axis=2)], axis=1)
    return f1, f4, jnp.asarray(gfwd.astype(np.float32), BF16)


DFT_ROWS = 2 * SUBLANE


def _kron_dot(f_ref, x3):
    k, _, ch = x3.shape
    halves = []
    for h in range(DFT_ROWS // SUBLANE):
        xh = x3[:, h * SUBLANE:(h + 1) * SUBLANE, :].reshape(k * SUBLANE, ch).astype(BF16)
        yh = jnp.dot(f_ref[...], xh, preferred_element_type=F32)
        halves.append(yh.reshape(-1, SUBLANE, ch))
    return jnp.concatenate(halves, axis=1)


def _dft1_kernel(x_ref, f_ref, o_ref):
    o_ref[0, :, 0] = _kron_dot(f_ref, x_ref[0, :, 0]).astype(o_ref.dtype)


def _dft1(x, f1, n):
    bsz, seq, ch = x.shape
    nb = n // DFT_ROWS
    return pl.pallas_call(
        _dft1_kernel,
        grid=(bsz, nb),
        in_specs=[pl.BlockSpec((1, n // 2, 1, DFT_ROWS, ch), lambda b, j: (b, 0, j, 0, 0)),
                  pl.BlockSpec(f1.shape, lambda b, j: (0, 0))],
        out_specs=pl.BlockSpec((1, 2 * n, 1, DFT_ROWS, ch), lambda b, j: (b, 0, j, 0, 0)),
        out_shape=jax.ShapeDtypeStruct((bsz, 2 * n, nb, DFT_ROWS, ch), BF16),
        compiler_params=_cp("parallel", "parallel"),
        name="dft_stage1",
    )(x.reshape(bsz, n // 2, nb, DFT_ROWS, ch), f1)


def _filt_spec_kernel(af_ref, ab_ref, g_ref, hre_ref, him_ref):
    n = g_ref.shape[1] // 2
    for i in range(g_ref.shape[0]):
        xf = jnp.dot(g_ref[i], af_ref[0, i], preferred_element_type=F32)
        xb = jnp.dot(g_ref[i], ab_ref[0, i], preferred_element_type=F32)
        hre_ref[i] = xf[:n] + xb[:n]
        him_ref[i] = xf[n:] - xb[n:]


def _conv_spec_kernel(a_ref, gf_ref, hre_ref, him_ref, c_ref):
    n = gf_ref.shape[1] // 2
    kb = gf_ref.shape[0]
    xs = [jnp.dot(gf_ref[i], a_ref[0, i], preferred_element_type=F32) for i in range(kb)]
    ps = []
    for i in range(kb):
        xr, xi = xs[i][:n], xs[i][n:]
        hr, hi = hre_ref[i], him_ref[i]
        ps.append(jnp.concatenate([xr * hr - xi * hi, xr * hi + xi * hr], axis=0).astype(BF16))
    for i in range(kb):
        c = lax.dot_general(gf_ref[i], ps[i], (((0,), (0,)), ((), ())), preferred_element_type=F32)
        c_ref[0, i] = c.astype(c_ref.dtype)


def _dft4_kernel(c_ref, f_ref, x0_ref, gv_ref, bias_ref, o_ref):
    y = _kron_dot(f_ref, c_ref[0, :, 0].astype(F32))
    o_ref[0, :, 0] = (x0_ref[0, :, 0] * (y + gv_ref[0, :, 0] * bias_ref[...])).astype(o_ref.dtype)


def _hyena(proj, tables, conv_w, conv_b, f_w1, f_b1, f_freq, f_w2, f_b2, f_w3, f_bias):
    bsz, seq, _ = proj.shape
    n = math.isqrt(2 * seq)
    assert n * n == 2 * seq and n % 2 == 0
    f1, f4, gfwd = tables
    ch = HY_W
    kb = min(8, n)
    x0c, gv = _hy_prep(proj, conv_w, conv_b)
    filt = _hy_filter(seq, f_w1, f_b1, f_freq, f_w2, f_b2, f_w3)

    a_f = _dft1(filt, f1, n).reshape(2, n, 2 * n, ch)
    adir = lambda bb: pl.BlockSpec((1, kb, 2 * n, ch), lambda k: (bb, k, 0, 0))
    gspec1 = pl.BlockSpec((kb, 2 * n, 2 * n), lambda k: (k, 0, 0))
    hspec1 = pl.BlockSpec((kb, n, ch), lambda k: (k, 0, 0))
    h_re, h_im = pl.pallas_call(
        _filt_spec_kernel,
        grid=(n // kb,),
        in_specs=[adir(0), adir(1), gspec1],
        out_specs=[hspec1, hspec1],
        out_shape=[jax.ShapeDtypeStruct((n, n, ch), F32)] * 2,
        compiler_params=_cp("parallel"),
        name="hyena_filter_spectrum",
    )(a_f, a_f, gfwd)

    a = _dft1(gv, f1, n).reshape(bsz, n, 2 * n, ch)
    aspec = pl.BlockSpec((1, kb, 2 * n, ch), lambda k, b: (b, k, 0, 0))
    gspec = pl.BlockSpec((kb, 2 * n, 2 * n), lambda k, b: (k, 0, 0))
    hspec = pl.BlockSpec((kb, n, ch), lambda k, b: (k, 0, 0))
    c = pl.pallas_call(
        _conv_spec_kernel,
        grid=(n // kb, bsz),
        in_specs=[aspec, gspec, hspec, hspec],
        out_specs=aspec,
        out_shape=jax.ShapeDtypeStruct((bsz, n, 2 * n, ch), BF16),
        compiler_params=_cp("parallel", "parallel"),
        name="hyena_conv_spectrum",
    )(a, gfwd, h_re, h_im).reshape(bsz, 2 * n, n // DFT_ROWS, DFT_ROWS, ch)

    nb = n // DFT_ROWS
    nat = lambda arr: arr.reshape(bsz, n // 2, nb, DFT_ROWS, ch)
    rspec = pl.BlockSpec((1, n // 2, 1, DFT_ROWS, ch), lambda b, j: (b, 0, j, 0, 0))
    out = pl.pallas_call(
        _dft4_kernel,
        grid=(bsz, nb),
        in_specs=[pl.BlockSpec((1, 2 * n, 1, DFT_ROWS, ch), lambda b, j: (b, 0, j, 0, 0)),
                  pl.BlockSpec(f4.shape, lambda b, j: (0, 0)),
                  rspec, rspec,
                  pl.BlockSpec((1, ch), lambda b, j: (0, 0))],
        out_specs=rspec,
        out_shape=jax.ShapeDtypeStruct((bsz, n // 2, nb, DFT_ROWS, ch), BF16),
        compiler_params=_cp("parallel", "parallel"),
        name="dft_stage4",
    )(c, f4, nat(x0c), nat(gv), f_bias.astype(F32).reshape(1, ch))
    return out.reshape(bsz, seq, ch)


def _outproj_kernel(om_ref, os_ref, oh_ref, x_ref, mod_ref, w_ref, g_ref, b_ref, o_ref):
    n_m, n_s = om_ref.shape[2], os_ref.shape[2]
    gate = mod_ref[0][2:3]
    half = x_ref.shape[1] // 2
    mixes = []
    for rows in (slice(0, half), slice(half, 2 * half)):
        mixed = jnp.dot(om_ref[0, rows], w_ref[0, 0:n_m], preferred_element_type=F32)
        mixed += jnp.dot(os_ref[0, rows], w_ref[0, n_m:n_m + n_s], preferred_element_type=F32)
        mixed += jnp.dot(oh_ref[0, rows], w_ref[0, n_m + n_s:], preferred_element_type=F32)
        mixes.append((rows, mixed))
    for rows, mixed in mixes:
        o_ref[0, rows] = _layer_norm(ALPHA * x_ref[0, rows] + gate * mixed, g_ref[...], b_ref[...])


def _outproj(o_mla, o_s5, o_hy, x, mod, w, l, ln_g, ln_b):
    bsz, seq, d = x.shape
    tm = min(512, seq)
    row = lambda width: pl.BlockSpec((1, tm, width), lambda b, i: (b, i, 0))
    const = lambda shp: pl.BlockSpec(shp, lambda b, i: (0,) * len(shp))
    return pl.pallas_call(
        _outproj_kernel,
        grid=(bsz, seq // tm),
        in_specs=[row(o_mla.shape[2]), row(o_s5.shape[2]), row(o_hy.shape[2]), row(d),
                  pl.BlockSpec((1, 6, d), lambda b, i: (b, 0, 0)),
                  pl.BlockSpec((1,) + w.shape[1:], lambda b, i: (l, 0, 0)), const((1, d)), const((1, d))],
        out_specs=row(d),
        out_shape=jax.ShapeDtypeStruct((bsz, seq, d), F32),
        compiler_params=_cp("parallel", "parallel"),
        name="out_proj_ln",
    )(o_mla, o_s5, o_hy, x, mod, w, ln_g.reshape(1, -1), ln_b.reshape(1, -1))


def _ffn_kernel(xp_ref, x_ref, xn_ref, mod_ref, wg_ref, wu_ref, cw_ref, cb_ref, wd_ref,
                g_ref, b_ref, o_ref, u_scr, acc_scr):
    i, j = pl.program_id(1), pl.program_id(2)
    tm = x_ref.shape[1]
    halo = xp_ref.shape[1]

    @pl.when(j == 0)
    def _():
        m = mod_ref[0]
        sc, sh = 1.0 + m[4:5], m[3:4]
        u_scr[0:halo] = (xp_ref[0] * sc + sh).astype(BF16)
        u_scr[halo:halo + tm] = (x_ref[0] * sc + sh).astype(BF16)
        u_scr[halo + tm:] = (xn_ref[0] * sc + sh).astype(BF16)
        acc_scr[...] = jnp.zeros_like(acc_scr)

    u = u_scr[...]
    rows = tm + 2 * halo
    gx = jnp.dot(u, wg_ref[0], preferred_element_type=F32)
    r = lax.broadcasted_iota(jnp.int32, (tm, 1), 0)
    keep_prev = jnp.logical_or(r > 0, i > 0)
    keep_next = jnp.logical_or(r < tm - 1, i < pl.num_programs(1) - 1)
    g_prev = jnp.where(keep_prev, pltpu.roll(gx, 1, 0)[halo:halo + tm], 0.0)
    g_next = jnp.where(keep_next, pltpu.roll(gx, rows - 1, 0)[halo:halo + tm], 0.0)
    cw = cw_ref[...]
    conv = g_prev * cw[0:1] + gx[halo:halo + tm] * cw[1:2] + g_next * cw[2:3] + cb_ref[...]
    up = jnp.dot(u[halo:halo + tm], wu_ref[0], preferred_element_type=F32)
    h = (conv * jax.nn.sigmoid(conv) * up).astype(BF16)
    acc_scr[...] += jnp.dot(h, wd_ref[0], preferred_element_type=F32)

    @pl.when(j == pl.num_programs(2) - 1)
    def _():
        gate = mod_ref[0][5:6]
        o_ref[0] = _layer_norm(ALPHA * x_ref[0] + gate * acc_scr[...], g_ref[...], b_ref[...])


def _ffn(x, mod, wg, wu, conv_w, conv_b, wd, l, ln_g, ln_b):
    bsz, seq, d = x.shape
    ff = wg.shape[2]
    tm = min(512, seq)
    tn = 512
    halo = SUBLANE
    nhb = seq // halo
    per = tm // halo
    const = lambda shp: pl.BlockSpec(shp, lambda b, i, j: (0,) * len(shp))
    return pl.pallas_call(
        _ffn_kernel,
        grid=(bsz, seq // tm, ff // tn),
        in_specs=[pl.BlockSpec((1, halo, d), lambda b, i, j: (b, jnp.maximum(i * per - 1, 0), 0)),
                  pl.BlockSpec((1, tm, d), lambda b, i, j: (b, i, 0)),
                  pl.BlockSpec((1, halo, d), lambda b, i, j: (b, jnp.minimum((i + 1) * per, nhb - 1), 0)),
                  pl.BlockSpec((1, 6, d), lambda b, i, j: (b, 0, 0)),
                  pl.BlockSpec((1, d, tn), lambda b, i, j: (l, 0, j)),
                  pl.BlockSpec((1, d, tn), lambda b, i, j: (l, 0, j)),
                  pl.BlockSpec((3, tn), lambda b, i, j: (0, j)),
                  pl.BlockSpec((1, tn), lambda b, i, j: (0, j)),
                  pl.BlockSpec((1, tn, d), lambda b, i, j: (l, j, 0)),
                  const((1, d)), const((1, d))],
        out_specs=pl.BlockSpec((1, tm, d), lambda b, i, j: (b, i, 0)),
        out_shape=jax.ShapeDtypeStruct((bsz, seq, d), F32),
        scratch_shapes=[pltpu.VMEM((tm + 2 * halo, d), BF16), pltpu.VMEM((tm, d), F32)],
        compiler_params=_cp("parallel", "parallel", "arbitrary"),
        name="conv_ffn_ln",
    )(x, x, x, mod, wg, wu, conv_w.astype(F32), conv_b.reshape(1, -1).astype(F32), wd,
      ln_g.reshape(1, -1), ln_b.reshape(1, -1))


def _rot_half_cols(w):
    half = w.shape[-1] // 2
    return jnp.concatenate([-w[..., half:], w[..., :half]], axis=-1)


def _prep_w_in(w):
    q, kv, kr, s5, hy = jnp.split(w, (512, 768, 832, 1344), axis=-1)
    return jnp.concatenate([q, hy, kv, kr, _rot_half_cols(kr), s5], axis=-1).astype(BF16)


def _prep_w_uq(w):
    w = w.reshape(w.shape[:-1] + (MLA_HEADS, MLA_QK))
    pe = w[..., MLA_NOPE:]
    return jnp.concatenate([w, _rot_half_cols(pe)], axis=-1).reshape(w.shape[:-2] + (-1,)).astype(BF16)


def kernel(x, c, positions, ada_w, ada_b, w_in, q_norm_g, kv_norm_g, w_uq, w_ukv, s5_a_re, s5_a_im, s5_log_step, s5_b_re, s5_b_im, s5_c_re, s5_c_im, s5_d, s5_glu_w, s5_glu_b, hy_conv_w, hy_conv_b, hy_f_w1, hy_f_b1, hy_f_freq, hy_f_w2, hy_f_b2, hy_f_w3, hy_f_bias, w_out, ln1_g, ln1_b, ffn_w_gate, ffn_w_up, ffn_conv_w, ffn_conv_b, ffn_w_down, ln2_g, ln2_b):
    bsz, seq, _ = x.shape
    depth = ada_w.shape[0]
    mods = _ada(c, ada_w, ada_b)
    tables = _dft_tables(math.isqrt(2 * seq))
    w_in_b = _prep_w_in(w_in)
    w_uq_b = _prep_w_uq(w_uq)
    w_ukv_b = w_ukv.astype(BF16)
    w_out_b = w_out.astype(BF16)
    wg_b, wu_b, wd_b = ffn_w_gate.astype(BF16), ffn_w_up.astype(BF16), ffn_w_down.astype(BF16)
    s5_all = jax.vmap(_s5_params)(s5_a_re, s5_a_im, s5_log_step, s5_b_re, s5_b_im, s5_c_re, s5_c_im, s5_d)
    for l in range(depth):
        mod = mods[l]
        proj, xs5 = _inproj(x, mod, w_in_b, l)
        q, k, v = _mla_prep(proj, positions, q_norm_g[l], kv_norm_g[l], w_uq_b[l], w_ukv_b[l])
        o_mla = _attention(q, k, v)
        o_s5 = _s5(xs5, s5_all, l, s5_glu_w[l], s5_glu_b[l])
        o_hy = _hyena(proj, tables, hy_conv_w[l], hy_conv_b[l], hy_f_w1[l], hy_f_b1[l],
                      hy_f_freq[l], hy_f_w2[l], hy_f_b2[l], hy_f_w3[l], hy_f_bias[l])
        x = _outproj(o_mla, o_s5, o_hy, x, mod, w_out_b, l, ln1_g[l], ln1_b[l])
        x = _ffn(x, mod, wg_b, wu_b, ffn_conv_w[l], ffn_conv_b[l], wd_b, l, ln2_g[l], ln2_b[l])
    return x
```
